```python
import jax, jax.numpy as jnp
from jax import lax
import numpy as np

D_MODEL = 2048
BATCH = 32
SEQ = 256
DEPTH = 2
DEC_BATCH = 4
DEC_SEQ = 1024
PAST_LEN = 256

GRID_W = 64
HEAD_DIM = 64
NA_HEADS = 16
NA_WIN_H = 8
NA_WIN_W = 16
FT_GROUPS = 4
FT_GROUP_DIM = 256
SW_HEADS = 16
SW_KV_HEADS = 4
SW_WINDOW = 128
SW_BLOCK = 128
ROPE_BASE = 10000.0
Q_BLOCK = 128
N_EXPERTS = 32
TOP_K = 4
D_FF = 2048
SWIGLU_LIMIT = 7.0
SWIGLU_ALPHA = 1.702
MOE_BLOCK = 128
EPS = 1e-6
NEG_INF = -1e30

NA_WIDTH = NA_HEADS * HEAD_DIM
FT_WIDTH = FT_GROUPS * FT_GROUP_DIM
SW_Q_WIDTH = SW_HEADS * HEAD_DIM
SW_KV_WIDTH = SW_KV_HEADS * HEAD_DIM
N_BRANCH = 3
IN_WIDTH = 3 * NA_WIDTH + FT_WIDTH + SW_Q_WIDTH + 2 * SW_KV_WIDTH + N_BRANCH * D_MODEL

kernel_name = 'hybrid_dit_na_fourier_swa_moe_step'


def rms_norm(x, g):
    xf = x.astype(jnp.float32)
    y = xf * lax.rsqrt(jnp.mean(xf * xf, axis=-1, keepdims=True) + EPS)
    return (y * g.astype(jnp.float32)).astype(x.dtype)


def _ada(cvec, w_ada, b_ada):
    m = jax.nn.silu(cvec) @ w_ada + b_ada
    return jnp.split(m[..., None, :], 6, axis=-1)


def _modulate(x, g, shift, scale):
    return rms_norm(x, g) * (1 + scale) + shift


def _heads(t, n):
    return t.reshape(t.shape[:-1] + (n, HEAD_DIM))


def _joint_softmax(parts):
    sizes = [p.shape[-1] for p in parts]
    p = jax.nn.softmax(jnp.concatenate(parts, axis=-1), axis=-1)
    cuts = [int(v) for v in np.cumsum(sizes)[:-1]]
    return jnp.split(p, cuts, axis=-1)


def _mixer_inputs(h, w_in, qn_na, kn_na, qn_sw, kn_sw):
    z = h @ w_in
    sizes = (NA_WIDTH, NA_WIDTH, NA_WIDTH, FT_WIDTH, SW_Q_WIDTH, SW_KV_WIDTH, SW_KV_WIDTH)
    cuts = [int(v) for v in np.cumsum(sizes)]
    qa, ka, va, ub, qc, kc, vc, gates = jnp.split(z, cuts, axis=-1)
    qa = rms_norm(_heads(qa, NA_HEADS), qn_na)
    ka = rms_norm(_heads(ka, NA_HEADS), kn_na)
    va = _heads(va, NA_HEADS)
    qc = rms_norm(_heads(qc, SW_HEADS), qn_sw)
    kc = rms_norm(_heads(kc, SW_KV_HEADS), kn_sw)
    vc = _heads(vc, SW_KV_HEADS)
    return qa, ka, va, ub, qc, kc, vc, gates


def _axial_rope(x):
    n = x.shape[1]
    t = jnp.arange(n)
    row = (t // GRID_W).astype(jnp.float32)
    col = (t % GRID_W).astype(jnp.float32)
    nf = HEAD_DIM // 4
    inv = ROPE_BASE ** (-jnp.arange(nf, dtype=jnp.float32) / nf)
    ang = jnp.concatenate([row[:, None] * inv, col[:, None] * inv], axis=-1)[None, :, None, :]
    cos, sin = jnp.cos(ang), jnp.sin(ang)
    x1, x2 = jnp.split(x.astype(jnp.float32), 2, axis=-1)
    return jnp.concatenate([x1 * cos - x2 * sin, x2 * cos + x1 * sin], axis=-1).astype(x.dtype)


def _ctx_attention(q, k, v, sink):
    B, S, Hq, d = q.shape
    Hkv = k.shape[2]
    G = Hq // Hkv
    scale = d ** -0.5
    qb = q.reshape(B, S // Q_BLOCK, Q_BLOCK, Hkv, G, d).transpose(1, 0, 2, 3, 4, 5)

    def one_block(qblk):
        s = jnp.einsum('bqkgd,bskd->bkgqs', qblk, k).astype(jnp.float32) * scale
        if sink is None:
            p = jax.nn.softmax(s, axis=-1)
        else:
            sk = sink.astype(jnp.float32).reshape(Hkv, G)[None, :, :, None, None]
            p, _ = _joint_softmax([s, jnp.broadcast_to(sk, s.shape[:-1] + (1,))])
        return jnp.einsum('bkgqs,bskd->bqkgd', p.astype(v.dtype), v)

    o = lax.map(one_block, qb)
    return o.transpose(1, 0, 2, 3, 4, 5).reshape(B, S, Hq * d)


def _neighbourhood_attention(q, k, v, k_ctx, v_ctx, rpb):
    B, N, H, d = q.shape
    rows = N // GRID_W
    wh = min(NA_WIN_H, rows)
    ww = NA_WIN_W
    scale = d ** -0.5
    r_start = jnp.clip(jnp.arange(rows) - wh // 2, 0, rows - wh)
    c_start = jnp.clip(jnp.arange(GRID_W) - ww // 2, 0, GRID_W - ww)
    col_idx = c_start[:, None] + jnp.arange(ww)
    dc = col_idx - jnp.arange(GRID_W)[:, None] + NA_WIN_W - 1
    bias_cols = rpb.astype(jnp.float32)[:, :, dc]
    kg = k.reshape(B, rows, GRID_W, H, d)
    vg = v.reshape(B, rows, GRID_W, H, d)
    q_rows = q.reshape(B, rows, GRID_W, H, d).transpose(1, 0, 2, 3, 4)

    def one_row(args):
        q_row, r0, r = args
        k_nb = lax.dynamic_slice_in_dim(kg, r0, wh, axis=1)[:, :, col_idx]
        v_nb = lax.dynamic_slice_in_dim(vg, r0, wh, axis=1)[:, :, col_idx]
        s = jnp.einsum('bchd,brcjhd->bhcrj', q_row, k_nb).astype(jnp.float32) * scale
        dr = r0 + jnp.arange(wh) - r + NA_WIN_H - 1
        s = s + bias_cols[:, dr].transpose(0, 2, 1, 3)[None]
        s_ctx = jnp.einsum('bchd,bphd->bhcp', q_row, k_ctx).astype(jnp.float32) * scale
        p_nb, p_ctx = _joint_softmax([s.reshape(B, H, GRID_W, wh * ww), s_ctx])
        p_nb = p_nb.reshape(B, H, GRID_W, wh, ww).astype(v.dtype)
        return (jnp.einsum('bhcrj,brcjhd->bchd', p_nb, v_nb)
                + jnp.einsum('bhcp,bphd->bchd', p_ctx.astype(v.dtype), v_ctx))

    o = lax.map(one_row, (q_rows, r_start, jnp.arange(rows)))
    return o.transpose(1, 0, 2, 3, 4).reshape(B, N, H * d)


def _window_attention(q, k, v, k_ctx, v_ctx, sink):
    B, N, Hq, d = q.shape
    Hkv = k.shape[2]
    G = Hq // Hkv
    scale = d ** -0.5
    nb = N // SW_BLOCK
    ns = -(-SW_WINDOW // SW_BLOCK)
    L = (2 * ns + 1) * SW_BLOCK
    padw = ((0, 0), (ns * SW_BLOCK, ns * SW_BLOCK), (0, 0), (0, 0))
    kp = jnp.pad(k, padw).reshape(B, nb + 2 * ns, SW_BLOCK, Hkv, d)
    vp = jnp.pad(v, padw).reshape(B, nb + 2 * ns, SW_BLOCK, Hkv, d)
    kw = jnp.concatenate([kp[:, j:j + nb] for j in range(2 * ns + 1)], axis=2)
    vw = jnp.concatenate([vp[:, j:j + nb] for j in range(2 * ns + 1)], axis=2)
    qb = q.reshape(B, nb, SW_BLOCK, Hkv, G, d)
    s = jnp.einsum('bnqkgd,bnskd->bnkgqs', qb, kw).astype(jnp.float32) * scale
    qpos = jnp.arange(nb)[:, None] * SW_BLOCK + jnp.arange(SW_BLOCK)
    kpos = jnp.arange(nb)[:, None] * SW_BLOCK - ns * SW_BLOCK + jnp.arange(L)
    kq = kpos[:, None, :]
    valid = (kq >= 0) & (kq < N) & (jnp.abs(kq - qpos[:, :, None]) <= SW_WINDOW)
    s = jnp.where(valid[None, :, None, None], s, NEG_INF)
    s_ctx = jnp.einsum('bnqkgd,bpkd->bnkgqp', qb, k_ctx).astype(jnp.float32) * scale
    sk = sink.astype(jnp.float32).reshape(Hkv, G)[None, None, :, :, None, None]
    p_w, p_c, _ = _joint_softmax([s, s_ctx, jnp.broadcast_to(sk, s.shape[:-1] + (1,))])
    o = (jnp.einsum('bnkgqs,bnskd->bnqkgd', p_w.astype(v.dtype), vw)
         + jnp.einsum('bnkgqp,bpkd->bnqkgd', p_c.astype(v.dtype), v_ctx))
    return o.reshape(B, N, Hq * d)


def _fourier_mix(u):
    B, L, _ = u.shape
    ug = u.reshape(B, L, FT_GROUPS, FT_GROUP_DIM).astype(jnp.float32)
    f = jnp.fft.fft2(ug, axes=(1, 3), norm='ortho').real
    return f.reshape(B, L, FT_WIDTH).astype(u.dtype)


def _merge(o_a, o_b, o_c, gates, w_br_na, w_br_ft, w_br_sw, w_o):
    g = jax.nn.sigmoid(gates.astype(jnp.float32)).astype(o_a.dtype)
    ga, gb, gc = jnp.split(g, N_BRANCH, axis=-1)
    m = ga * (o_a @ w_br_na) + gb * (o_b @ w_br_ft) + gc * (o_c @ w_br_sw)
    return m @ w_o


def _moe(x, w_router, b_router, w_gu, b_gu, w_down, b_down):
    shp = x.shape
    xt = x.reshape(-1, D_MODEL)
    T = xt.shape[0]
    logits = (xt @ w_router + b_router).astype(jnp.float32)
    top_v, top_i = lax.top_k(logits, TOP_K)
    gate = jax.nn.softmax(top_v, axis=-1)
    A = T * TOP_K
    e_flat = top_i.reshape(-1)
    tok_flat = jnp.arange(A) // TOP_K
    g_flat = gate.reshape(-1)
    order = jnp.argsort(e_flat)
    e_sorted = e_flat[order]
    tok_sorted = tok_flat[order]
    counts = jnp.zeros((N_EXPERTS,), jnp.int32).at[e_flat].add(1)
    starts = jnp.cumsum(counts) - counts
    padded = (counts + MOE_BLOCK - 1) // MOE_BLOCK * MOE_BLOCK
    pad_ends = jnp.cumsum(padded)
    pad_starts = pad_ends - padded
    dest = pad_starts[e_sorted] + (jnp.arange(A) - starts[e_sorted])
    n_rows = (A + N_EXPERTS * (MOE_BLOCK - 1) + MOE_BLOCK - 1) // MOE_BLOCK * MOE_BLOCK
    n_blocks = n_rows // MOE_BLOCK
    buf = jnp.zeros((n_rows, D_MODEL), xt.dtype).at[dest].set(xt[tok_sorted])
    blk_e = jnp.minimum(jnp.searchsorted(pad_ends, jnp.arange(n_blocks) * MOE_BLOCK, side='right'),
                        N_EXPERTS - 1)

    def expert_block(args):
        xb, e = args
        gu = xb @ w_gu[e] + b_gu[e]
        x_glu, x_lin = jnp.split(gu, 2, axis=-1)
        x_glu = jnp.minimum(x_glu, SWIGLU_LIMIT)
        x_lin = jnp.clip(x_lin, -SWIGLU_LIMIT, SWIGLU_LIMIT)
        hdn = x_glu * jax.nn.sigmoid(SWIGLU_ALPHA * x_glu) * (x_lin + 1)
        return hdn @ w_down[e] + b_down[e]

    out = lax.map(expert_block, (buf.reshape(n_blocks, MOE_BLOCK, D_MODEL), blk_e)).reshape(n_rows, D_MODEL)
    contrib = out[dest] * g_flat[order][:, None].astype(out.dtype)
    y = jax.ops.segment_sum(contrib, tok_sorted, num_segments=T)
    return y.reshape(shp)


def setup_inputs(seed: int = 0) -> dict:
    key = jax.random.key(seed)
    ks = jax.random.split(key, 29)
    D, L, E, F, hd = D_MODEL, DEPTH, N_EXPERTS, D_FF, HEAD_DIM

    def nrm(i, shape, scale):
        return jax.random.normal(ks[i], shape, jnp.float32) * scale

    return {
        'x_prompt': nrm(0, (BATCH, SEQ, D), 1.0),
        'x_sample': nrm(1, (DEC_BATCH, DEC_SEQ, D), 1.0),
        'cache_na_k': nrm(2, (DEC_BATCH, L, PAST_LEN, NA_HEADS, hd), 1.0),
        'cache_na_v': nrm(3, (DEC_BATCH, L, PAST_LEN, NA_HEADS, hd), 1.0),
        'cache_sw_k': nrm(4, (DEC_BATCH, L, PAST_LEN, SW_KV_HEADS, hd), 1.0),
        'cache_sw_v': nrm(5, (DEC_BATCH, L, PAST_LEN, SW_KV_HEADS, hd), 1.0),
        'c': nrm(6, (DEC_BATCH, D), 1.0),
        'c_ctx': nrm(7, (D,), 1.0),
        'w_ada': nrm(8, (L, D, 6 * D), 0.5 * D ** -0.5),
        'b_ada': nrm(9, (L, 6 * D), 0.02),
        'norm1_g': 1.0 + nrm(10, (L, D), 0.02),
        'norm2_g': 1.0 + nrm(11, (L, D), 0.02),
        'w_in': nrm(12, (L, D, IN_WIDTH), D ** -0.5),
        'qn_na': 1.0 + nrm(13, (L, hd), 0.02),
        'kn_na': 1.0 + nrm(14, (L, hd), 0.02),
        'qn_sw': 1.0 + nrm(15, (L, hd), 0.02),
        'kn_sw': 1.0 + nrm(16, (L, hd), 0.02),
        'rpb_na': nrm(17, (L, NA_HEADS, 2 * NA_WIN_H - 1, 2 * NA_WIN_W - 1), 0.1),
        'sink_sw': nrm(18, (L, SW_HEADS), 0.5),
        'w_br_na': nrm(19, (L, NA_WIDTH, D), NA_WIDTH ** -0.5),
        'w_br_ft': nrm(20, (L, FT_WIDTH, D), FT_WIDTH ** -0.5),
        'w_br_sw': nrm(21, (L, SW_Q_WIDTH, D), SW_Q_WIDTH ** -0.5),
        'w_o': nrm(22, (L, D, D), D ** -0.5),
        'w_router': nrm(23, (L, D, E), D ** -0.5),
        'b_router': nrm(24, (L, E), 0.01),
        'w_gu': nrm(25, (L, E, D, 2 * F), D ** -0.5),
        'b_gu': nrm(26, (L, E, 2 * F), 0.01),
        'w_down': nrm(27, (L, E, F, D), F ** -0.5),
        'b_down': nrm(28, (L, E, D), 0.01),
    }


def reference(x_prompt, x_sample, cache_na_k, cache_na_v, cache_sw_k, cache_sw_v, c, c_ctx,
              w_ada, b_ada, norm1_g, norm2_g, w_in, qn_na, kn_na, qn_sw, kn_sw, rpb_na, sink_sw,
              w_br_na, w_br_ft, w_br_sw, w_o, w_router, b_router, w_gu, b_gu, w_down, b_down):
    xp = x_prompt
    xs = x_sample
    na_k, na_v, sw_k, sw_v = [], [], [], []
    for l in range(DEPTH):
        sh1, sc1, g1, sh2, sc2, g2 = _ada(c_ctx, w_ada[l], b_ada[l])
        h = _modulate(xp, norm1_g[l], sh1, sc1)
        qa, ka, va, ub, qc, kc, vc, gates = _mixer_inputs(h, w_in[l], qn_na[l], kn_na[l], qn_sw[l], kn_sw[l])
        o_a = _ctx_attention(qa, ka, va, None)
        o_b = _fourier_mix(ub)
        o_c = _ctx_attention(qc, kc, vc, sink_sw[l])
        xp = xp + g1 * _merge(o_a, o_b, o_c, gates, w_br_na[l], w_br_ft[l], w_br_sw[l], w_o[l])
        h = _modulate(xp, norm2_g[l], sh2, sc2)
        xp = xp + g2 * _moe(h, w_router[l], b_router[l], w_gu[l], b_gu[l], w_down[l], b_down[l])
        na_k.append(ka)
        na_v.append(va)
        sw_k.append(kc)
        sw_v.append(vc)

        sh1, sc1, g1, sh2, sc2, g2 = _ada(c, w_ada[l], b_ada[l])
        h = _modulate(xs, norm1_g[l], sh1, sc1)
        qa, ka, va, ub, qc, kc, vc, gates = _mixer_inputs(h, w_in[l], qn_na[l], kn_na[l], qn_sw[l], kn_sw[l])
        o_a = _neighbourhood_attention(qa, ka, va, cache_na_k[:, l], cache_na_v[:, l], rpb_na[l])
        o_b = _fourier_mix(ub)
        o_c = _window_attention(_axial_rope(qc), _axial_rope(kc), vc,
                                cache_sw_k[:, l], cache_sw_v[:, l], sink_sw[l])
        xs = xs + g1 * _merge(o_a, o_b, o_c, gates, w_br_na[l], w_br_ft[l], w_br_sw[l], w_o[l])
        h = _modulate(xs, norm2_g[l], sh2, sc2)
        xs = xs + g2 * _moe(h, w_router[l], b_router[l], w_gu[l], b_gu[l], w_down[l], b_down[l])

    new_na_k = jnp.stack(na_k, axis=1)
    new_na_v = jnp.stack(na_v, axis=1)
    new_sw_k = jnp.stack(sw_k, axis=1)
    new_sw_v = jnp.stack(sw_v, axis=1)
    return (xp, xs, new_na_k, new_na_v, new_sw_k, new_sw_v)
```

```python
import functools

import numpy as np
import jax
import jax.numpy as jnp
from jax import lax
from jax.experimental import pallas as pl
from jax.experimental.pallas import tpu as pltpu

F32 = jnp.float32
BF16 = jnp.bfloat16

D_MODEL = 2048
DEPTH = 2
CTX_BATCH = 32
CTX_SEQ = 256
LAT_BATCH = 4
LAT_SEQ = 1024
PAST_LEN = 256
T_CTX = CTX_BATCH * CTX_SEQ
T_LAT = LAT_BATCH * LAT_SEQ
T_ALL = T_CTX + T_LAT
GRID_W = 64
HEAD_DIM = 64
NA_HEADS = 16
NA_WIN_H = 8
NA_WIN_W = 16
FT_GROUPS = 4
FT_GROUP_DIM = 256
SW_HEADS = 16
SW_KV_HEADS = 4
SW_WINDOW = 128
ROPE_BASE = 10000.0
N_EXPERTS = 32
TOP_K = 4
D_FF = 2048
SWIGLU_LIMIT = 7.0
SWIGLU_ALPHA = 1.702
EPS = 1e-6
NEG_INF = -1e30

Z_QA, Z_KA, Z_VA, Z_UB, Z_QC, Z_KC, Z_VC, Z_GATE = 0, 1024, 2048, 3072, 4096, 5120, 5376, 5632
IN_WIDTH = Z_GATE + 3 * D_MODEL
N_MOD = 6 * D_MODEL
CTX_MOD_ROW = LAT_BATCH
MOD_ROWS = 8

LANES = 128
VMEM_LIMIT = 56 * 1024 * 1024

MOE_SUB = 256
MOE_UNIT = 1024
MOE_NSUB = MOE_UNIT // MOE_SUB
MOE_FCHUNK = 256
MOE_NCHUNK = D_FF // MOE_FCHUNK
N_ASSIGN = T_ALL * TOP_K
MOE_ROWS = -(-(N_ASSIGN + N_EXPERTS * (MOE_SUB - 1)) // MOE_UNIT) * MOE_UNIT
MOE_UNITS = MOE_ROWS // MOE_UNIT + N_EXPERTS
ROUTER_TM = 256
COMBINE_TM = 256


def _params(**kw):
    return pltpu.CompilerParams(vmem_limit_bytes=VMEM_LIMIT, **kw)


def _mod_row(row0):
    return jnp.where(row0 < T_CTX, CTX_MOD_ROW, (row0 - T_CTX) // LAT_SEQ)


def _split(a):
    hi = a.astype(BF16)
    lo = (a - hi.astype(F32)).astype(BF16)
    return hi, lo


def _dot3(a, b, dims):
    ah, al = _split(a)
    bh, bl = _split(b)
    d = functools.partial(lax.dot_general, dimension_numbers=dims, preferred_element_type=F32)
    return d(ah, bh) + d(ah, bl) + d(al, bh)


_NN = (((1,), (0,)), ((), ()))
_NT = (((1,), (1,)), ((), ()))


def _ada_kernel(c_ref, w_ref, b_ref, o_ref):
    c = c_ref[...]
    s = c * jax.nn.sigmoid(c)
    o_ref[...] = _dot3(s, w_ref[...], _NN) + b_ref[...]


def _ada(cvecs, w_ada, b_ada):
    tn = 1024
    return pl.pallas_call(
        _ada_kernel,
        out_shape=jax.ShapeDtypeStruct((DEPTH, MOD_ROWS, N_MOD), F32),
        grid=(DEPTH, N_MOD // tn),
        in_specs=[
            pl.BlockSpec((MOD_ROWS, D_MODEL), lambda l, j: (0, 0)),
            pl.BlockSpec((None, D_MODEL, tn), lambda l, j: (l, 0, j)),
            pl.BlockSpec((None, 1, tn), lambda l, j: (l, 0, j)),
        ],
        out_specs=pl.BlockSpec((None, MOD_ROWS, tn), lambda l, j: (l, 0, j)),
        compiler_params=_params(),
        name="ada_table",
    )(cvecs, w_ada, b_ada.reshape(DEPTH, 1, N_MOD))


def _rms_mod(x, g, shift, scale):
    y = x * lax.rsqrt(jnp.mean(x * x, axis=-1, keepdims=True) + EPS)
    return (y * g) * (1.0 + scale) + shift


def _prenorm_kernel(x_ref, g_ref, sh_ref, sc_ref, h_ref):
    h_ref[...] = _rms_mod(x_ref[...], g_ref[...], sh_ref[...], sc_ref[...]).astype(h_ref.dtype)


def _prenorm(x, g, mods):
    tm = 512
    return pl.pallas_call(
        _prenorm_kernel,
        out_shape=jax.ShapeDtypeStruct((T_ALL, D_MODEL), BF16),
        grid=(T_ALL // tm,),
        in_specs=[
            pl.BlockSpec((tm, D_MODEL), lambda i: (i, 0)),
            pl.BlockSpec((1, D_MODEL), lambda i: (0, 0)),
            pl.BlockSpec((None, 1, D_MODEL), lambda i: (_mod_row(i * tm), 0, 0)),
            pl.BlockSpec((None, 1, D_MODEL), lambda i: (_mod_row(i * tm), 0, 1)),
        ],
        out_specs=pl.BlockSpec((tm, D_MODEL), lambda i: (i, 0)),
        compiler_params=_params(),
        name="prenorm",
    )(x, g.reshape(1, D_MODEL), mods, mods)


def _mm_kernel(a_ref, w_ref, o_ref):
    o_ref[...] = jnp.dot(a_ref[...], w_ref[...].astype(BF16),
                         preferred_element_type=F32).astype(o_ref.dtype)


def _in_proj(h, w_in, layer):
    tm, tn = 1024, 512
    return pl.pallas_call(
        _mm_kernel,
        out_shape=jax.ShapeDtypeStruct((T_ALL, IN_WIDTH), F32),
        grid=(T_ALL // tm, IN_WIDTH // tn),
        in_specs=[
            pl.BlockSpec((tm, D_MODEL), lambda i, j: (i, 0)),
            pl.BlockSpec((None, D_MODEL, tn), lambda i, j: (layer, 0, j)),
        ],
        out_specs=pl.BlockSpec((tm, tn), lambda i, j: (i, j)),
        compiler_params=_params(),
        name="in_proj",
    )(h, w_in)


def _head_norm(x, gain, bd):
    hi, lo = _split(x * x)
    ss = jnp.dot(hi, bd, preferred_element_type=F32) + jnp.dot(lo, bd, preferred_element_type=F32)
    return x * lax.rsqrt(ss * (1.0 / HEAD_DIM) + EPS) * gain


def _prep_kernel(qa_ref, ka_ref, qc_ref, kc_ref, vc_ref, gqa_ref, gka_ref, gqc_ref, gkc_ref,
                 bd_ref, cos_ref, sin_ref,
                 qa_o, ka_o, qc_o, kcn_o, kcd_o, vcd_o, *, tm):
    is_lat = pl.program_id(0) * tm >= T_CTX
    lane = lax.broadcasted_iota(jnp.int32, (1, LANES), 1)
    first_half = (lane % HEAD_DIM) < (HEAD_DIM // 2)
    low_head = lane < HEAD_DIM
    bd = bd_ref[...]
    cos = jnp.where(is_lat, cos_ref[...], 1.0)
    sin = jnp.where(is_lat, sin_ref[...], 0.0)

    def rope(y):
        partner = jnp.where(first_half, pltpu.roll(y, LANES - HEAD_DIM // 2, 1),
                            pltpu.roll(y, HEAD_DIM // 2, 1))
        return y * cos + partner * sin

    for s in range(NA_HEADS * HEAD_DIM // LANES):
        sl = slice(s * LANES, (s + 1) * LANES)
        qa_o[:, sl] = _head_norm(qa_ref[:, sl], gqa_ref[...], bd)
        ka_o[:, sl] = _head_norm(ka_ref[:, sl], gka_ref[...], bd)
        qc_o[:, sl] = rope(_head_norm(qc_ref[:, sl], gqc_ref[...], bd))
    for s in range(SW_KV_HEADS * HEAD_DIM // LANES):
        sl = slice(s * LANES, (s + 1) * LANES)
        kn = _head_norm(kc_ref[:, sl], gkc_ref[...], bd)
        kcn_o[:, sl] = kn
        kr = rope(kn)
        v = vc_ref[:, sl]
        kr_sw = pltpu.roll(kr, HEAD_DIM, 1)
        v_sw = pltpu.roll(v, HEAD_DIM, 1)
        kcd_o[:, (2 * s) * LANES:(2 * s + 1) * LANES] = jnp.where(low_head, kr, kr_sw)
        kcd_o[:, (2 * s + 1) * LANES:(2 * s + 2) * LANES] = jnp.where(low_head, kr_sw, kr)
        vcd_o[:, (2 * s) * LANES:(2 * s + 1) * LANES] = jnp.where(low_head, v, v_sw)
        vcd_o[:, (2 * s + 1) * LANES:(2 * s + 2) * LANES] = jnp.where(low_head, v_sw, v)


def _prep(z, gqa, gka, gqc, gkc, bd, cos_t, sin_t):
    tm = 256
    nq = NA_HEADS * HEAD_DIM
    nkv = SW_KV_HEADS * HEAD_DIM
    lat_tiles = LAT_SEQ // tm

    def pos_map(i):
        return (jnp.maximum(i - T_CTX // tm, 0) % lat_tiles, 0)

    gain = pl.BlockSpec((1, LANES), lambda i: (0, 0))
    return pl.pallas_call(
        functools.partial(_prep_kernel, tm=tm),
        out_shape=(
            jax.ShapeDtypeStruct((T_ALL, nq), F32),
            jax.ShapeDtypeStruct((T_ALL, nq), F32),
            jax.ShapeDtypeStruct((T_ALL, nq), F32),
            jax.ShapeDtypeStruct((T_ALL, nkv), F32),
            jax.ShapeDtypeStruct((T_ALL, 2 * nkv), F32),
            jax.ShapeDtypeStruct((T_ALL, 2 * nkv), F32),
        ),
        grid=(T_ALL // tm,),
        in_specs=[
            pl.BlockSpec((tm, nq), lambda i: (i, Z_QA // nq)),
            pl.BlockSpec((tm, nq), lambda i: (i, Z_KA // nq)),
            pl.BlockSpec((tm, nq), lambda i: (i, Z_QC // nq)),
            pl.BlockSpec((tm, nkv), lambda i: (i, Z_KC // nkv)),
            pl.BlockSpec((tm, nkv), lambda i: (i, Z_VC // nkv)),
            gain, gain, gain, gain,
            pl.BlockSpec((LANES, LANES), lambda i: (0, 0)),
            pl.BlockSpec((tm, LANES), pos_map),
            pl.BlockSpec((tm, LANES), pos_map),
        ],
        out_specs=(
            pl.BlockSpec((tm, nq), lambda i: (i, 0)),
            pl.BlockSpec((tm, nq), lambda i: (i, 0)),
            pl.BlockSpec((tm, nq), lambda i: (i, 0)),
            pl.BlockSpec((tm, nkv), lambda i: (i, 0)),
            pl.BlockSpec((tm, 2 * nkv), lambda i: (i, 0)),
            pl.BlockSpec((tm, 2 * nkv), lambda i: (i, 0)),
        ),
        compiler_params=_params(),
        name="qk_prep",
    )(z, z, z, z, z, gqa, gka, gqc, gkc, bd, cos_t, sin_t)


def _attn_kernel(*refs, tq, tk, has_cache, has_bias, has_sink, band, aliased):
    it = iter(refs)
    q_ref, k_ref, v_ref = next(it), next(it), next(it)
    kc_ref = vc_ref = bias_ref = sink_ref = None
    if has_cache:
        kc_ref, vc_ref = next(it), next(it)
    if has_bias:
        bias_ref = next(it)
    if has_sink:
        sink_ref = next(it)
    if aliased:
        next(it)
    o_ref = next(it)

    pair = pl.program_id(0)
    qi = pl.program_id(1)
    lane = lax.broadcasted_iota(jnp.int32, (1, LANES), 1)
    q = q_ref[...] * (HEAD_DIM ** -0.5)
    k = k_ref[...].astype(BF16)
    v = v_ref[...]
    if has_cache:
        kc = kc_ref[...].astype(BF16)
        vc = vc_ref[...]
    if band:
        qpos = qi * tq + lax.broadcasted_iota(jnp.int32, (tq, 1), 0)
        kpos = lax.broadcasted_iota(jnp.int32, (1, tk), 1)
        in_band = jnp.abs(qpos - kpos) <= SW_WINDOW

    acc = jnp.zeros((tq, LANES), F32)
    for half in range(2):
        hm = (lane < HEAD_DIM) if half == 0 else (lane >= HEAD_DIM)
        qm = jnp.where(hm, q, 0.0).astype(BF16)
        s = lax.dot_general(qm, k, _NT, preferred_element_type=F32)
        if has_bias:
            s = s + bias_ref[half]
        if band:
            s = jnp.where(in_band, s, NEG_INF)
        m = jnp.max(s, axis=1, keepdims=True)
        if has_cache:
            sc = lax.dot_general(qm, kc, _NT, preferred_element_type=F32)
            m = jnp.maximum(m, jnp.max(sc, axis=1, keepdims=True))
        if has_sink:
            sk = sink_ref[2 * pair + half]
            m = jnp.maximum(m, sk)
        p = jnp.exp(s - m)
        denom = jnp.sum(p, axis=1, keepdims=True)
        o = jnp.dot(p.astype(BF16), jnp.where(hm, v, 0.0).astype(BF16), preferred_element_type=F32)
        if has_cache:
            pc = jnp.exp(sc - m)
            denom = denom + jnp.sum(pc, axis=1, keepdims=True)
            o = o + jnp.dot(pc.astype(BF16), jnp.where(hm, vc, 0.0).astype(BF16),
                            preferred_element_type=F32)
        if has_sink:
            denom = denom + jnp.exp(sk - m)
        acc = acc + o / denom
    o_ref[...] = acc.astype(o_ref.dtype)


def _attention(q, k, v, *, n_batch, seq, row0, tq, q_col0=0, k_col0=0, v_col0=0, kv_div=1,
               cache_k=None, cache_v=None, bias=None, sink=None, band=False, out_prev=None, name):
    n_pairs = NA_HEADS * HEAD_DIM // LANES
    nqt = seq // tq
    has_cache, has_bias, has_sink = cache_k is not None, bias is not None, sink is not None
    aliased = out_prev is not None

    def q_map(p, qi, b):
        return (row0 // tq + b * nqt + qi, q_col0 + p)

    in_specs = [
        pl.BlockSpec((tq, LANES), q_map),
        pl.BlockSpec((seq, LANES), lambda p, qi, b: (row0 // seq + b, k_col0 + p // kv_div)),
        pl.BlockSpec((seq, LANES), lambda p, qi, b: (row0 // seq + b, v_col0 + p // kv_div)),
    ]
    args = [q, k, v]
    if has_cache:
        cspec = pl.BlockSpec((None, PAST_LEN, LANES), lambda p, qi, b: (b, 0, p // kv_div))
        in_specs += [cspec, cspec]
        args += [cache_k, cache_v]
    if has_bias:
        in_specs.append(pl.BlockSpec((None, 2, tq, seq), lambda p, qi, b: (p, 0, qi, 0)))
        args.append(bias)
    if has_sink:
        in_specs.append(pl.BlockSpec(memory_space=pltpu.SMEM))
        args.append(sink)
    if aliased:
        in_specs.append(pl.BlockSpec(memory_space=pl.ANY))
        args.append(out_prev)
    return pl.pallas_call(
        functools.partial(_attn_kernel, tq=tq, tk=seq, has_cache=has_cache, has_bias=has_bias,
                          has_sink=has_sink, band=band, aliased=aliased),
        out_shape=jax.ShapeDtypeStruct((T_ALL, NA_HEADS * HEAD_DIM), BF16),
        grid=(n_pairs, nqt, n_batch),
        in_specs=in_specs,
        out_specs=pl.BlockSpec((tq, LANES), lambda p, qi, b: (row0 // tq + b * nqt + qi, p)),
        input_output_aliases={len(args) - 1: 0} if aliased else {},
        compiler_params=_params(),
        name=name,
    )(*args)


def _fourier_kernel(*refs, aliased):
    u_ref, cc_ref, sc_ref, cl_ref, sl_ref = refs[:5]
    o_ref = refs[-1]
    u = u_ref[...].astype(BF16)
    a = jnp.dot(u, cc_ref[...], preferred_element_type=F32).astype(BF16)
    b = jnp.dot(u, sc_ref[...], preferred_element_type=F32).astype(BF16)
    o = (jnp.dot(cl_ref[...], a, preferred_element_type=F32)
         + jnp.dot(sl_ref[...], b, preferred_element_type=F32))
    o_ref[...] = o.astype(o_ref.dtype)


def _fourier(z, cc, sc, cl, sl, *, n_batch, seq, row0, out_prev=None, name):
    aliased = out_prev is not None
    gd = FT_GROUP_DIM
    in_specs = [
        pl.BlockSpec((seq, gd), lambda b, g: (row0 // seq + b, Z_UB // gd + g)),
        pl.BlockSpec((gd, gd), lambda b, g: (0, 0)),
        pl.BlockSpec((gd, gd), lambda b, g: (0, 0)),
        pl.BlockSpec((seq, seq), lambda b, g: (0, 0)),
        pl.BlockSpec((seq, seq), lambda b, g: (0, 0)),
    ]
    args = [z, cc, sc, cl, sl]
    if aliased:
        in_specs.append(pl.BlockSpec(memory_space=pl.ANY))
        args.append(out_prev)
    return pl.pallas_call(
        functools.partial(_fourier_kernel, aliased=aliased),
        out_shape=jax.ShapeDtypeStruct((T_ALL, FT_GROUPS * gd), BF16),
        grid=(n_batch, FT_GROUPS),
        in_specs=in_specs,
        out_specs=pl.BlockSpec((seq, gd), lambda b, g: (row0 // seq + b, g)),
        input_output_aliases={len(args) - 1: 0} if aliased else {},
        compiler_params=_params(),
        name=name,
    )(*args)


def _merge1_kernel(oa, ob, oc, wa, wb, wc, ga, gb, gc, m_ref):
    def branch(o, w, g):
        return jax.nn.sigmoid(g[...]) * jnp.dot(o[...], w[...].astype(BF16),
                                                preferred_element_type=F32)
    m_ref[...] = (branch(oa, wa, ga) + branch(ob, wb, gb) + branch(oc, wc, gc)).astype(m_ref.dtype)


def _merge1(o_a, o_b, o_c, w_a, w_b, w_c, z, layer):
    tm, tn = 1024, 512
    kdim = NA_HEADS * HEAD_DIM
    o_spec = pl.BlockSpec((tm, kdim), lambda i, j: (i, 0))
    w_spec = pl.BlockSpec((None, kdim, tn), lambda i, j: (layer, 0, j))

    def gate_spec(br):
        return pl.BlockSpec((tm, tn), lambda i, j: (i, (Z_GATE + br * D_MODEL) // tn + j))

    return pl.pallas_call(
        _merge1_kernel,
        out_shape=jax.ShapeDtypeStruct((T_ALL, D_MODEL), BF16),
        grid=(T_ALL // tm, D_MODEL // tn),
        in_specs=[o_spec, o_spec, o_spec, w_spec, w_spec, w_spec,
                  gate_spec(0), gate_spec(1), gate_spec(2)],
        out_specs=pl.BlockSpec((tm, tn), lambda i, j: (i, j)),
        compiler_params=_params(),
        name="merge_branches",
    )(o_a, o_b, o_c, w_a, w_b, w_c, z, z, z)


def _merge2_kernel(m_ref, w_ref, x_ref, g_ref, o_ref):
    o_ref[...] = x_ref[...] + g_ref[...] * jnp.dot(m_ref[...], w_ref[...].astype(BF16),
                                                   preferred_element_type=F32)


def _merge2(m, w_o, x, mods, layer):
    tm, tn = 1024, 512
    return pl.pallas_call(
        _merge2_kernel,
        out_shape=jax.ShapeDtypeStruct((T_ALL, D_MODEL), F32),
        grid=(T_ALL // tm, D_MODEL // tn),
        in_specs=[
            pl.BlockSpec((tm, D_MODEL), lambda i, j: (i, 0)),
            pl.BlockSpec((None, D_MODEL, tn), lambda i, j: (layer, 0, j)),
            pl.BlockSpec((tm, tn), lambda i, j: (i, j)),
            pl.BlockSpec((None, 1, tn), lambda i, j: (_mod_row(i * tm), 0, 2 * D_MODEL // tn + j)),
        ],
        out_specs=pl.BlockSpec((tm, tn), lambda i, j: (i, j)),
        compiler_params=_params(),
        name="out_proj_residual",
    )(m, w_o, x, mods)


def _router_kernel(x_ref, g_ref, sh_ref, sc_ref, wr_ref, br_ref, tri_ref,
                   h_ref, e_ref, gate_ref, rank_ref, cnt_ref, carry):
    @pl.when(pl.program_id(0) == 0)
    def _():
        carry[...] = jnp.zeros_like(carry)

    h = _rms_mod(x_ref[...], g_ref[...], sh_ref[...], sc_ref[...])
    h_ref[...] = h
    tm = h.shape[0]
    logits = _dot3(wr_ref[...], h, _NT) + br_ref[...]
    eidx = lax.broadcasted_iota(jnp.int32, (N_EXPERTS, tm), 0).astype(F32)
    work = logits
    vals, hots = [], []
    for kk in range(TOP_K):
        m = jnp.max(work, axis=0, keepdims=True)
        idx = jnp.min(jnp.where(work == m, eidx, float(N_EXPERTS)), axis=0, keepdims=True)
        hot = eidx == idx
        work = jnp.where(hot, -jnp.inf, work)
        vals.append(m)
        hots.append(hot)
        e_ref[kk:kk + 1, :] = idx.astype(jnp.int32)
    ex = [jnp.exp(vv - vals[0]) for vv in vals]
    den = ex[0] + ex[1] + ex[2] + ex[3]
    for kk in range(TOP_K):
        gate_ref[kk:kk + 1, :] = ex[kk] / den
    chosen = jnp.zeros((N_EXPERTS, tm), F32)
    for hot in hots:
        chosen = chosen + jnp.where(hot, 1.0, 0.0)
    before = jnp.dot(chosen.astype(BF16), tri_ref[...], preferred_element_type=F32) + carry[...]
    for kk in range(TOP_K):
        rk = jnp.sum(jnp.where(hots[kk], before, 0.0), axis=0, keepdims=True)
        rank_ref[kk:kk + 1, :] = rk.astype(jnp.int32)
    carry[...] = carry[...] + jnp.sum(chosen, axis=1, keepdims=True)
    cnt_ref[...] = carry[...]


def _router(x, g, mods, w_router_t, b_router, tri):
    tm = ROUTER_TM
    return pl.pallas_call(
        _router_kernel,
        out_shape=(
            jax.ShapeDtypeStruct((T_ALL, D_MODEL), F32),
            jax.ShapeDtypeStruct((TOP_K, T_ALL), jnp.int32),
            jax.ShapeDtypeStruct((TOP_K, T_ALL), F32),
            jax.ShapeDtypeStruct((TOP_K, T_ALL), jnp.int32),
            jax.ShapeDtypeStruct((N_EXPERTS, 1), F32),
        ),
        grid=(T_ALL // tm,),
        in_specs=[
            pl.BlockSpec((tm, D_MODEL), lambda i: (i, 0)),
            pl.BlockSpec((1, D_MODEL), lambda i: (0, 0)),
            pl.BlockSpec((None, 1, D_MODEL), lambda i: (_mod_row(i * tm), 0, 3)),
            pl.BlockSpec((None, 1, D_MODEL), lambda i: (_mod_row(i * tm), 0, 4)),
            pl.BlockSpec((N_EXPERTS, D_MODEL), lambda i: (0, 0)),
            pl.BlockSpec((N_EXPERTS, 1), lambda i: (0, 0)),
            pl.BlockSpec((tm, tm), lambda i: (0, 0)),
        ],
        out_specs=(
            pl.BlockSpec((tm, D_MODEL), lambda i: (i, 0)),
            pl.BlockSpec((TOP_K, tm), lambda i: (0, i)),
            pl.BlockSpec((TOP_K, tm), lambda i: (0, i)),
            pl.BlockSpec((TOP_K, tm), lambda i: (0, i)),
            pl.BlockSpec((N_EXPERTS, 1), lambda i: (0, 0)),
        ),
        scratch_shapes=[pltpu.VMEM((N_EXPERTS, 1), F32)],
        compiler_params=_params(dimension_semantics=("arbitrary",)),
        name="norm2_router",
    )(x, g.reshape(1, D_MODEL), mods, mods, w_router_t, b_router.reshape(N_EXPERTS, 1), tri)


def _moe_kernel(ue_ref, urow_ref, un_ref, uf_ref, ua_ref, src_ref,
                h_hbm, wg_ref, wl_ref, bg_ref, bl_ref, wd_ref, bd_ref, out_hbm,
                xf, xb, acc, gsem, osem):
    u = pl.program_id(0)
    c = pl.program_id(1)
    nrows = un_ref[u]
    nfill = uf_ref[u]
    row0 = urow_ref[u]

    def row_copy(r, tok):
        return pltpu.make_async_copy(h_hbm.at[pl.ds(tok, 1)], xf.at[pl.ds(r, 1)], gsem)

    def out_copy(sub):
        rows = pl.ds(sub * MOE_SUB, MOE_SUB)
        dst = pl.ds(pl.multiple_of(row0 + sub * MOE_SUB, MOE_SUB), MOE_SUB)
        return pltpu.make_async_copy(acc.at[rows], out_hbm.at[dst], osem)

    @pl.when(c == 0)
    def _():
        def issue(r, carry):
            row_copy(r, src_ref[row0 + r]).start()
            return carry

        def drain(r, carry):
            row_copy(r, 0).wait()
            return carry

        lax.fori_loop(0, nrows, issue, 0)
        lax.fori_loop(0, nrows, drain, 0)
        for sub in range(MOE_NSUB):
            @pl.when(sub * MOE_SUB < nrows)
            def _():
                rows = pl.ds(sub * MOE_SUB, MOE_SUB)
                xb[rows, :] = xf[rows, :].astype(BF16)

    wg = wg_ref[...].astype(BF16)
    wl = wl_ref[...].astype(BF16)
    wd = wd_ref[...].astype(BF16)
    for sub in range(MOE_NSUB):
        @pl.when(sub * MOE_SUB < nrows)
        def _():
            rows = pl.ds(sub * MOE_SUB, MOE_SUB)
            xs = xb[rows, :]
            glu = jnp.dot(xs, wg, preferred_element_type=F32) + bg_ref[...]
            lin = jnp.dot(xs, wl, preferred_element_type=F32) + bl_ref[...]
            glu = jnp.minimum(glu, SWIGLU_LIMIT)
            lin = jnp.clip(lin, -SWIGLU_LIMIT, SWIGLU_LIMIT)
            hdn = glu * jax.nn.sigmoid(SWIGLU_ALPHA * glu) * (lin + 1.0)
            part = jnp.dot(hdn.astype(BF16), wd, preferred_element_type=F32)

            @pl.when(c == 0)
            def _():
                acc[rows, :] = part + bd_ref[...]

            @pl.when(c > 0)
            def _():
                acc[rows, :] = acc[rows, :] + part

    @pl.when(c == MOE_NCHUNK - 1)
    def _():
        @pl.when(nfill > 0)
        def _():
            acc[...] = jnp.zeros_like(acc)

        nout = nrows + nfill
        for sub in range(MOE_NSUB):
            @pl.when(sub * MOE_SUB < nout)
            def _():
                out_copy(sub).start()
        for sub in range(MOE_NSUB):
            @pl.when(sub * MOE_SUB < nout)
            def _():
                out_copy(sub).wait()


def _moe(h2, plan, w_gu, b_gu, w_down, b_down, layer):
    fc = MOE_FCHUNK

    def chunk(c, act, u):
        return jnp.where(act[u] > 0, c, MOE_NCHUNK - 1)

    def wg_map(u, c, ue, ur, un, uf, ua, sr):
        return (layer, ue[u], 0, chunk(c, ua, u))

    def wl_map(u, c, ue, ur, un, uf, ua, sr):
        return (layer, ue[u], 0, MOE_NCHUNK + chunk(c, ua, u))

    def wd_map(u, c, ue, ur, un, uf, ua, sr):
        return (layer, ue[u], chunk(c, ua, u), 0)

    def bd_map(u, c, ue, ur, un, uf, ua, sr):
        return (layer, ue[u], 0, 0)

    grid_spec = pltpu.PrefetchScalarGridSpec(
        num_scalar_prefetch=6,
        grid=(MOE_UNITS, MOE_NCHUNK),
        in_specs=[
            pl.BlockSpec(memory_space=pl.ANY),
            pl.BlockSpec((None, None, D_MODEL, fc), wg_map),
            pl.BlockSpec((None, None, D_MODEL, fc), wl_map),
            pl.BlockSpec((None, None, 1, fc), wg_map),
            pl.BlockSpec((None, None, 1, fc), wl_map),
            pl.BlockSpec((None, None, fc, D_MODEL), wd_map),
            pl.BlockSpec((None, None, 1, D_MODEL), bd_map),
        ],
        out_specs=pl.BlockSpec(memory_space=pl.ANY),
        scratch_shapes=[
            pltpu.VMEM((MOE_UNIT, D_MODEL), F32),
            pltpu.VMEM((MOE_UNIT, D_MODEL), BF16),
            pltpu.VMEM((MOE_UNIT, D_MODEL), F32),
            pltpu.SemaphoreType.DMA,
            pltpu.SemaphoreType.DMA,
        ],
    )
    return pl.pallas_call(
        _moe_kernel,
        out_shape=jax.ShapeDtypeStruct((MOE_ROWS, D_MODEL), F32),
        grid_spec=grid_spec,
        compiler_params=_params(dimension_semantics=("arbitrary", "arbitrary")),
        name="moe_experts",
    )(*plan, h2, w_gu, w_gu,
      b_gu.reshape(DEPTH, N_EXPERTS, 1, 2 * D_FF), b_gu.reshape(DEPTH, N_EXPERTS, 1, 2 * D_FF),
      w_down, b_down.reshape(DEPTH, N_EXPERTS, 1, D_MODEL))


def _combine_kernel(dest_ref, rows_hbm, gate_ref, x_ref, g2_ref, o_ref, buf, sem):
    tm = COMBINE_TM
    base = pl.program_id(0) * tm

    def row_copy(r, kk, d):
        return pltpu.make_async_copy(rows_hbm.at[pl.ds(d, 1)], buf.at[kk, pl.ds(r, 1)], sem)

    def issue(r, carry):
        for kk in range(TOP_K):
            row_copy(r, kk, dest_ref[(base + r) * TOP_K + kk]).start()
        return carry

    def drain(r, carry):
        for kk in range(TOP_K):
            row_copy(r, kk, 0).wait()
        return carry

    lax.fori_loop(0, tm, issue, 0)
    lax.fori_loop(0, tm, drain, 0)
    gates = gate_ref[...]
    y = gates[:, 0:1] * buf[0]
    for kk in range(1, TOP_K):
        y = y + gates[:, kk:kk + 1] * buf[kk]
    o_ref[...] = x_ref[...] + g2_ref[...] * y


def _combine(dest_flat, rows, gates_t, x, mods):
    tm = COMBINE_TM
    grid_spec = pltpu.PrefetchScalarGridSpec(
        num_scalar_prefetch=1,
        grid=(T_ALL // tm,),
        in_specs=[
            pl.BlockSpec(memory_space=pl.ANY),
            pl.BlockSpec((tm, TOP_K), lambda i, d: (i, 0)),
            pl.BlockSpec((tm, D_MODEL), lambda i, d: (i, 0)),
            pl.BlockSpec((None, 1, D_MODEL), lambda i, d: (_mod_row(i * tm), 0, 5)),
        ],
        out_specs=pl.BlockSpec((tm, D_MODEL), lambda i, d: (i, 0)),
        scratch_shapes=[pltpu.VMEM((TOP_K, tm, D_MODEL), F32), pltpu.SemaphoreType.DMA],
    )
    return pl.pallas_call(
        _combine_kernel,
        out_shape=jax.ShapeDtypeStruct((T_ALL, D_MODEL), F32),
        grid_spec=grid_spec,
        compiler_params=_params(dimension_semantics=("arbitrary",)),
        name="moe_combine",
    )(dest_flat, rows, gates_t, x, mods)


def _moe_plan(e4, r4, counts):
    cnt = counts[:, 0].astype(jnp.int32)
    pad = (cnt + MOE_SUB - 1) // MOE_SUB * MOE_SUB
    pad_end = jnp.cumsum(pad)
    pad_start = pad_end - pad
    dest = pad_start[e4] + r4
    n_units_e = (pad + MOE_UNIT - 1) // MOE_UNIT
    unit_end = jnp.cumsum(n_units_e)
    unit_start = unit_end - n_units_e
    n_units = unit_end[-1]
    uid = jnp.minimum(jnp.arange(MOE_UNITS, dtype=jnp.int32), n_units - 1)
    unit_e = jnp.minimum(jnp.searchsorted(unit_end, uid, side='right'), N_EXPERTS - 1).astype(jnp.int32)
    local = uid - unit_start[unit_e]
    unit_row = pad_start[unit_e] + local * MOE_UNIT
    unit_act = (jnp.arange(MOE_UNITS, dtype=jnp.int32) < n_units).astype(jnp.int32)
    unit_n = jnp.clip(pad[unit_e] - local * MOE_UNIT, 0, MOE_UNIT) * unit_act
    fill_row = pad_end[-1] + (jnp.arange(MOE_UNITS, dtype=jnp.int32) - n_units) * MOE_UNIT
    unit_fill = jnp.clip(MOE_ROWS - fill_row, 0, MOE_UNIT) * (1 - unit_act)
    unit_row = jnp.where(unit_act > 0, unit_row, jnp.minimum(fill_row, MOE_ROWS - MOE_SUB))
    tok = jnp.broadcast_to(jnp.arange(T_ALL, dtype=jnp.int32)[None, :], (TOP_K, T_ALL))
    src = jnp.zeros((MOE_ROWS,), jnp.int32).at[dest.reshape(-1)].set(tok.reshape(-1))
    i32 = lambda a: a.astype(jnp.int32)
    return dest, (unit_e, i32(unit_row), i32(unit_n), i32(unit_fill), unit_act, src)


def _dft_tables(n):
    idx = np.outer(np.arange(n), np.arange(n)) % n
    ang = 2.0 * np.pi * idx / n
    s = 1.0 / np.sqrt(n)
    return np.cos(ang) * s, np.sin(ang) * s


def _rope_tables():
    t = np.arange(LAT_SEQ)
    nf = HEAD_DIM // 4
    inv = ROPE_BASE ** (-np.arange(nf, dtype=np.float64) / nf)
    ang = np.concatenate([(t // GRID_W)[:, None] * inv, (t % GRID_W)[:, None] * inv], axis=-1)
    lane = np.arange(LANES)
    cos = np.cos(ang)[:, lane % (HEAD_DIM // 2)]
    sin = np.sin(ang)[:, lane % (HEAD_DIM // 2)]
    sign = np.where((lane % HEAD_DIM) < HEAD_DIM // 2, -1.0, 1.0)
    return cos.astype(np.float32), (sin * sign).astype(np.float32)


def _na_bias(rpb):
    rows = LAT_SEQ // GRID_W
    wh = min(NA_WIN_H, rows)
    r = np.arange(rows)
    c = np.arange(GRID_W)
    r0 = np.clip(r - wh // 2, 0, rows - wh)
    c0 = np.clip(c - NA_WIN_W // 2, 0, GRID_W - NA_WIN_W)
    r_ok = (r[None, :] >= r0[:, None]) & (r[None, :] < r0[:, None] + wh)
    c_ok = (c[None, :] >= c0[:, None]) & (c[None, :] < c0[:, None] + NA_WIN_W)
    dr = np.clip(r[None, :] - r[:, None] + NA_WIN_H - 1, 0, 2 * NA_WIN_H - 2)
    dc = np.clip(c[None, :] - c[:, None] + NA_WIN_W - 1, 0, 2 * NA_WIN_W - 2)
    dr4 = np.broadcast_to(dr[:, None, :, None], (rows, GRID_W, rows, GRID_W)).reshape(LAT_SEQ, LAT_SEQ)
    dc4 = np.broadcast_to(dc[None, :, None, :], (rows, GRID_W, rows, GRID_W)).reshape(LAT_SEQ, LAT_SEQ)
    ok = (r_ok[:, None, :, None] & c_ok[None, :, None, :]).reshape(LAT_SEQ, LAT_SEQ)
    flat = rpb.astype(F32).reshape(NA_HEADS, -1)
    table = jnp.take(flat, jnp.asarray(dr4 * (2 * NA_WIN_W - 1) + dc4, jnp.int32), axis=1)
    table = jnp.where(jnp.asarray(ok)[None], table, NEG_INF)
    return table.reshape(NA_HEADS // 2, 2, LAT_SEQ, LAT_SEQ)


def _dup_heads(cache):
    b, p = cache.shape[:2]
    return jnp.broadcast_to(cache[:, :, :, None, :], (b, p, SW_KV_HEADS, 2, HEAD_DIM)).reshape(
        b, p, SW_KV_HEADS * LANES)


def kernel(x_prompt, x_sample, cache_na_k, cache_na_v, cache_sw_k, cache_sw_v, c, c_ctx,
           w_ada, b_ada, norm1_g, norm2_g, w_in, qn_na, kn_na, qn_sw, kn_sw, rpb_na, sink_sw,
           w_br_na, w_br_ft, w_br_sw, w_o, w_router, b_router, w_gu, b_gu, w_down, b_down):
    nq = NA_HEADS * HEAD_DIM
    x = jnp.concatenate([x_prompt.reshape(T_CTX, D_MODEL), x_sample.reshape(T_LAT, D_MODEL)], axis=0)
    cvecs = jnp.zeros((MOD_ROWS, D_MODEL), F32).at[:LAT_BATCH].set(c).at[CTX_MOD_ROW].set(c_ctx)
    mods_all = _ada(cvecs, w_ada, b_ada)

    cc, sc = _dft_tables(FT_GROUP_DIM)
    cl_ctx, sl_ctx = _dft_tables(CTX_SEQ)
    cl_lat, sl_lat = _dft_tables(LAT_SEQ)
    as_bf16 = lambda a: jnp.asarray(a, BF16)
    cos_t, sin_t = _rope_tables()
    bd = as_bf16(np.kron(np.eye(LANES // HEAD_DIM), np.ones((HEAD_DIM, HEAD_DIM))))
    tri = as_bf16(np.triu(np.ones((ROUTER_TM, ROUTER_TM)), 1))
    tile2 = lambda g: jnp.tile(g.reshape(1, HEAD_DIM), (1, LANES // HEAD_DIM))

    na_k, na_v, sw_k, sw_v = [], [], [], []
    for l in range(DEPTH):
        mods = mods_all[l].reshape(MOD_ROWS, 1, N_MOD)
        h = _prenorm(x, norm1_g[l], mods)
        z = _in_proj(h, w_in, l)
        qa, ka, qc, kc_n, kc_dup, vc_dup = _prep(z, tile2(qn_na[l]), tile2(kn_na[l]), tile2(qn_sw[l]),
                                                 tile2(kn_sw[l]), bd, jnp.asarray(cos_t),
                                                 jnp.asarray(sin_t))
        na_k.append(ka[:T_CTX].reshape(CTX_BATCH, CTX_SEQ, NA_HEADS, HEAD_DIM))
        na_v.append(z[:T_CTX, Z_VA:Z_VA + nq].reshape(CTX_BATCH, CTX_SEQ, NA_HEADS, HEAD_DIM))
        sw_k.append(kc_n[:T_CTX].reshape(CTX_BATCH, CTX_SEQ, SW_KV_HEADS, HEAD_DIM))
        sw_v.append(z[:T_CTX, Z_VC:Z_VC + SW_KV_HEADS * HEAD_DIM].reshape(
            CTX_BATCH, CTX_SEQ, SW_KV_HEADS, HEAD_DIM))

        ctx = dict(n_batch=CTX_BATCH, seq=CTX_SEQ, row0=0, tq=CTX_SEQ)
        lat = dict(n_batch=LAT_BATCH, seq=LAT_SEQ, row0=T_CTX, tq=256)
        o_init = jnp.zeros((T_ALL, nq), BF16)
        o_a = _attention(qa, ka, z, v_col0=Z_VA // LANES, out_prev=o_init, name="attn_na_ctx", **ctx)
        o_a = _attention(qa, ka, z, v_col0=Z_VA // LANES,
                         cache_k=cache_na_k[:, l].reshape(LAT_BATCH, PAST_LEN, nq),
                         cache_v=cache_na_v[:, l].reshape(LAT_BATCH, PAST_LEN, nq),
                         bias=_na_bias(rpb_na[l]), out_prev=o_a, name="attn_na_lat", **lat)
        o_c = _attention(qc, kc_dup, vc_dup, kv_div=2, sink=sink_sw[l], out_prev=o_init,
                         name="attn_sw_ctx", **ctx)
        o_c = _attention(qc, kc_dup, vc_dup, kv_div=2, sink=sink_sw[l],
                         cache_k=_dup_heads(cache_sw_k[:, l]), cache_v=_dup_heads(cache_sw_v[:, l]),
                         band=True, out_prev=o_c, name="attn_sw_lat", **lat)
        o_b = _fourier(z, as_bf16(cc), as_bf16(-sc), as_bf16(cl_ctx), as_bf16(sl_ctx),
                       n_batch=CTX_BATCH, seq=CTX_SEQ, row0=0, out_prev=o_init, name="fourier_ctx")
        o_b = _fourier(z, as_bf16(cc), as_bf16(-sc), as_bf16(cl_lat), as_bf16(sl_lat),
                       n_batch=LAT_BATCH, seq=LAT_SEQ, row0=T_CTX, out_prev=o_b, name="fourier_lat")
        m = _merge1(o_a, o_b, o_c, w_br_na, w_br_ft, w_br_sw, z, l)
        x = _merge2(m, w_o, x, mods, l)

        h2, e4, g4, r4, counts = _router(x, norm2_g[l], mods, w_router[l].T, b_router[l], tri)
        dest, plan = _moe_plan(e4, r4, counts)
        rows = _moe(h2, plan, w_gu, b_gu, w_down, b_down, l)
        x = _combine(dest.T.reshape(-1), rows, g4.T, x, mods)

    y_prompt = x[:T_CTX].reshape(CTX_BATCH, CTX_SEQ, D_MODEL)
    y_sample = x[T_CTX:].reshape(LAT_BATCH, LAT_SEQ, D_MODEL)
    return (y_prompt, y_sample, jnp.stack(na_k, axis=1), jnp.stack(na_v, axis=1),
            jnp.stack(sw_k, axis=1), jnp.stack(sw_v, axis=1))
```

```python
import functools

import numpy as np
import jax
import jax.numpy as jnp
from jax import lax
from jax.experimental import pallas as pl
from jax.experimental.pallas import tpu as pltpu

F32 = jnp.float32
BF16 = jnp.bfloat16

D_MODEL = 2048
DEPTH = 2
CTX_BATCH = 32
CTX_SEQ = 256
LAT_BATCH = 4
LAT_SEQ = 1024
PAST_LEN = 256
T_CTX = CTX_BATCH * CTX_SEQ
T_LAT = LAT_BATCH * LAT_SEQ
T_ALL = T_CTX + T_LAT
GRID_W = 64
HEAD_DIM = 64
NA_HEADS = 16
NA_WIN_H = 8
NA_WIN_W = 16
FT_GROUPS = 4
FT_GROUP_DIM = 256
SW_HEADS = 16
SW_KV_HEADS = 4
SW_WINDOW = 128
ROPE_BASE = 10000.0
N_EXPERTS = 32
TOP_K = 4
D_FF = 2048
SWIGLU_LIMIT = 7.0
SWIGLU_ALPHA = 1.702
EPS = 1e-6
NEG_INF = -1e30

Z_QA, Z_KA, Z_VA, Z_UB, Z_QC, Z_KC, Z_VC, Z_GATE = 0, 1024, 2048, 3072, 4096, 5120, 5376, 5632
IN_WIDTH = Z_GATE + 3 * D_MODEL
N_MOD = 6 * D_MODEL
CTX_MOD_ROW = LAT_BATCH
MOD_ROWS = 8

LANES = 128
VMEM_LIMIT = 56 * 1024 * 1024

MOE_SUB = 256
MOE_UNIT = 1024
MOE_NSUB = MOE_UNIT // MOE_SUB
MOE_FCHUNK = 512
MOE_NCHUNK = D_FF // MOE_FCHUNK
N_ASSIGN = T_ALL * TOP_K
MOE_ROWS = -(-(N_ASSIGN + N_EXPERTS * (MOE_SUB - 1)) // MOE_UNIT) * MOE_UNIT
MOE_UNITS = MOE_ROWS // MOE_UNIT + N_EXPERTS
ROUTER_TM = 256
COMBINE_TM = 256


def _params(**kw):
    return pltpu.CompilerParams(vmem_limit_bytes=VMEM_LIMIT, **kw)


def _mod_row(row0):
    return jnp.where(row0 < T_CTX, CTX_MOD_ROW, (row0 - T_CTX) // LAT_SEQ)


def _split(a):
    hi = a.astype(BF16)
    lo = (a - hi.astype(F32)).astype(BF16)
    return hi, lo


def _dot3(a, b, dims):
    ah, al = _split(a)
    bh, bl = _split(b)
    d = functools.partial(lax.dot_general, dimension_numbers=dims, preferred_element_type=F32)
    return d(ah, bh) + d(ah, bl) + d(al, bh)


_NN = (((1,), (0,)), ((), ()))
_NT = (((1,), (1,)), ((), ()))


def _ada_kernel(c_ref, w_ref, b_ref, o_ref):
    c = c_ref[...]
    s = c * jax.nn.sigmoid(c)
    o_ref[...] = _dot3(s, w_ref[...], _NN) + b_ref[...]


def _ada(cvecs, w_ada, b_ada):
    tn = 1024
    return pl.pallas_call(
        _ada_kernel,
        out_shape=jax.ShapeDtypeStruct((DEPTH, MOD_ROWS, N_MOD), F32),
        grid=(DEPTH, N_MOD // tn),
        in_specs=[
            pl.BlockSpec((MOD_ROWS, D_MODEL), lambda l, j: (0, 0)),
            pl.BlockSpec((None, D_MODEL, tn), lambda l, j: (l, 0, j)),
            pl.BlockSpec((None, 1, tn), lambda l, j: (l, 0, j)),
        ],
        out_specs=pl.BlockSpec((None, MOD_ROWS, tn), lambda l, j: (l, 0, j)),
        compiler_params=_params(),
        name="ada_table",
    )(cvecs, w_ada, b_ada.reshape(DEPTH, 1, N_MOD))


def _rms_mod(x, g, shift, scale):
    y = x * lax.rsqrt(jnp.mean(x * x, axis=-1, keepdims=True) + EPS)
    return (y * g) * (1.0 + scale) + shift


def _prenorm_kernel(x_ref, g_ref, sh_ref, sc_ref, h_ref):
    h_ref[...] = _rms_mod(x_ref[...], g_ref[...], sh_ref[...], sc_ref[...]).astype(h_ref.dtype)


def _prenorm(x, g, mods):
    tm = 512
    return pl.pallas_call(
        _prenorm_kernel,
        out_shape=jax.ShapeDtypeStruct((T_ALL, D_MODEL), BF16),
        grid=(T_ALL // tm,),
        in_specs=[
            pl.BlockSpec((tm, D_MODEL), lambda i: (i, 0)),
            pl.BlockSpec((1, D_MODEL), lambda i: (0, 0)),
            pl.BlockSpec((None, 1, D_MODEL), lambda i: (_mod_row(i * tm), 0, 0)),
            pl.BlockSpec((None, 1, D_MODEL), lambda i: (_mod_row(i * tm), 0, 1)),
        ],
        out_specs=pl.BlockSpec((tm, D_MODEL), lambda i: (i, 0)),
        compiler_params=_params(),
        name="prenorm",
    )(x, g.reshape(1, D_MODEL), mods, mods)


def _mm_kernel(a_ref, w_ref, o_ref):
    o_ref[...] = jnp.dot(a_ref[...], w_ref[...].astype(BF16),
                         preferred_element_type=F32).astype(o_ref.dtype)


def _in_proj(h, w_in, layer):
    tm, tn = 2048, 512
    return pl.pallas_call(
        _mm_kernel,
        out_shape=jax.ShapeDtypeStruct((T_ALL, IN_WIDTH), F32),
        grid=(T_ALL // tm, IN_WIDTH // tn),
        in_specs=[
            pl.BlockSpec((tm, D_MODEL), lambda i, j: (i, 0)),
            pl.BlockSpec((None, D_MODEL, tn), lambda i, j: (layer, 0, j)),
        ],
        out_specs=pl.BlockSpec((tm, tn), lambda i, j: (i, j)),
        compiler_params=_params(),
        name="in_proj",
    )(h, w_in)


def _head_norm(x, gain, bd):
    hi, lo = _split(x * x)
    ss = jnp.dot(hi, bd, preferred_element_type=F32) + jnp.dot(lo, bd, preferred_element_type=F32)
    return x * lax.rsqrt(ss * (1.0 / HEAD_DIM) + EPS) * gain


def _prep_kernel(qa_ref, ka_ref, qc_ref, kc_ref, vc_ref, gqa_ref, gka_ref, gqc_ref, gkc_ref,
                 bd_ref, cos_ref, sin_ref,
                 qa_o, ka_o, qc_o, kcn_o, kcd_o, vcd_o, *, tm):
    is_lat = pl.program_id(0) * tm >= T_CTX
    lane = lax.broadcasted_iota(jnp.int32, (1, LANES), 1)
    first_half = (lane % HEAD_DIM) < (HEAD_DIM // 2)
    low_head = lane < HEAD_DIM
    bd = bd_ref[...]
    cos = jnp.where(is_lat, cos_ref[...], 1.0)
    sin = jnp.where(is_lat, sin_ref[...], 0.0)

    def rope(y):
        partner = jnp.where(first_half, pltpu.roll(y, LANES - HEAD_DIM // 2, 1),
                            pltpu.roll(y, HEAD_DIM // 2, 1))
        return y * cos + partner * sin

    for s in range(NA_HEADS * HEAD_DIM // LANES):
        sl = slice(s * LANES, (s + 1) * LANES)
        qa_o[:, sl] = _head_norm(qa_ref[:, sl], gqa_ref[...], bd)
        ka_o[:, sl] = _head_norm(ka_ref[:, sl], gka_ref[...], bd)
        qc_o[:, sl] = rope(_head_norm(qc_ref[:, sl], gqc_ref[...], bd))
    for s in range(SW_KV_HEADS * HEAD_DIM // LANES):
        sl = slice(s * LANES, (s + 1) * LANES)
        kn = _head_norm(kc_ref[:, sl], gkc_ref[...], bd)
        kcn_o[:, sl] = kn
        kr = rope(kn)
        v = vc_ref[:, sl]
        kr_sw = pltpu.roll(kr, HEAD_DIM, 1)
        v_sw = pltpu.roll(v, HEAD_DIM, 1)
        kcd_o[:, (2 * s) * LANES:(2 * s + 1) * LANES] = jnp.where(low_head, kr, kr_sw)
        kcd_o[:, (2 * s + 1) * LANES:(2 * s + 2) * LANES] = jnp.where(low_head, kr_sw, kr)
        vcd_o[:, (2 * s) * LANES:(2 * s + 1) * LANES] = jnp.where(low_head, v, v_sw)
        vcd_o[:, (2 * s + 1) * LANES:(2 * s + 2) * LANES] = jnp.where(low_head, v_sw, v)


def _prep(z, gqa, gka, gqc, gkc, bd, cos_t, sin_t):
    tm = 256
    nq = NA_HEADS * HEAD_DIM
    nkv = SW_KV_HEADS * HEAD_DIM
    lat_tiles = LAT_SEQ // tm

    def pos_map(i):
        return (jnp.maximum(i - T_CTX // tm, 0) % lat_tiles, 0)

    gain = pl.BlockSpec((1, LANES), lambda i: (0, 0))
    return pl.pallas_call(
        functools.partial(_prep_kernel, tm=tm),
        out_shape=(
            jax.ShapeDtypeStruct((T_ALL, nq), F32),
            jax.ShapeDtypeStruct((T_ALL, nq), F32),
            jax.ShapeDtypeStruct((T_ALL, nq), F32),
            jax.ShapeDtypeStruct((T_ALL, nkv), F32),
            jax.ShapeDtypeStruct((T_ALL, 2 * nkv), F32),
            jax.ShapeDtypeStruct((T_ALL, 2 * nkv), F32),
        ),
        grid=(T_ALL // tm,),
        in_specs=[
            pl.BlockSpec((tm, nq), lambda i: (i, Z_QA // nq)),
            pl.BlockSpec((tm, nq), lambda i: (i, Z_KA // nq)),
            pl.BlockSpec((tm, nq), lambda i: (i, Z_QC // nq)),
            pl.BlockSpec((tm, nkv), lambda i: (i, Z_KC // nkv)),
            pl.BlockSpec((tm, nkv), lambda i: (i, Z_VC // nkv)),
            gain, gain, gain, gain,
            pl.BlockSpec((LANES, LANES), lambda i: (0, 0)),
            pl.BlockSpec((tm, LANES), pos_map),
            pl.BlockSpec((tm, LANES), pos_map),
        ],
        out_specs=(
            pl.BlockSpec((tm, nq), lambda i: (i, 0)),
            pl.BlockSpec((tm, nq), lambda i: (i, 0)),
            pl.BlockSpec((tm, nq), lambda i: (i, 0)),
            pl.BlockSpec((tm, nkv), lambda i: (i, 0)),
            pl.BlockSpec((tm, 2 * nkv), lambda i: (i, 0)),
            pl.BlockSpec((tm, 2 * nkv), lambda i: (i, 0)),
        ),
        compiler_params=_params(),
        name="qk_prep",
    )(z, z, z, z, z, gqa, gka, gqc, gkc, bd, cos_t, sin_t)


def _attn_kernel(*refs, tq, tk, pp, kv_div, has_cache, has_bias, has_sink, band, aliased):
    it = iter(refs)
    q_ref, k_ref, v_ref = next(it), next(it), next(it)
    kc_ref = vc_ref = bias_ref = sink_ref = None
    if has_cache:
        kc_ref, vc_ref = next(it), next(it)
    if has_bias:
        bias_ref = next(it)
    if has_sink:
        sink_ref = next(it)
    if aliased:
        next(it)
    o_ref = next(it)
    bias_scr = next(it) if has_bias else None

    group = pl.program_id(0)
    qi = pl.program_id(1)
    lane = lax.broadcasted_iota(jnp.int32, (1, LANES), 1)
    if band:
        qpos = qi * tq + lax.broadcasted_iota(jnp.int32, (tq, 1), 0)
        kpos = lax.broadcasted_iota(jnp.int32, (1, tk), 1)
        in_band = jnp.abs(qpos - kpos) <= SW_WINDOW
    if has_bias:
        @pl.when(pl.program_id(2) == 0)
        def _():
            _na_bias_tile(bias_ref, bias_scr, qi, tq, pp, lane)

    for j in range(pp):
        qs = slice(j * LANES, (j + 1) * LANES)
        ks = slice((j // kv_div) * LANES, (j // kv_div + 1) * LANES)
        q = q_ref[:, qs] * (HEAD_DIM ** -0.5)
        k = k_ref[:, ks].astype(BF16)
        v = v_ref[:, ks]
        if has_cache:
            kc = kc_ref[:, ks].astype(BF16)
            vc = vc_ref[:, ks]
        acc = jnp.zeros((tq, LANES), F32)
        for half in range(2):
            hm = (lane < HEAD_DIM) if half == 0 else (lane >= HEAD_DIM)
            qm = jnp.where(hm, q, 0.0).astype(BF16)
            s = lax.dot_general(qm, k, _NT, preferred_element_type=F32)
            if has_bias:
                s = s + bias_scr[j, half]
            if band:
                s = jnp.where(in_band, s, NEG_INF)
            m = jnp.max(s, axis=1, keepdims=True)
            if has_cache:
                sc = lax.dot_general(qm, kc, _NT, preferred_element_type=F32)
                m = jnp.maximum(m, jnp.max(sc, axis=1, keepdims=True))
            if has_sink:
                sk = sink_ref[2 * (group * pp + j) + half]
                m = jnp.maximum(m, sk)
            p = jnp.exp(s - m)
            denom = jnp.sum(p, axis=1, keepdims=True)
            o = jnp.dot(p.astype(BF16), jnp.where(hm, v, 0.0).astype(BF16),
                        preferred_element_type=F32)
            if has_cache:
                pc = jnp.exp(sc - m)
                denom = denom + jnp.sum(pc, axis=1, keepdims=True)
                o = o + jnp.dot(pc.astype(BF16), jnp.where(hm, vc, 0.0).astype(BF16),
                                preferred_element_type=F32)
            if has_sink:
                denom = denom + jnp.exp(sk - m)
            acc = acc + o / denom
        o_ref[:, qs] = acc.astype(o_ref.dtype)


def _na_bias_tile(tab_ref, out_ref, qi, tq, pp, lane):
    rows = LAT_SEQ // GRID_W
    wh = min(NA_WIN_H, rows)
    for a in range(tq // GRID_W):
        rq = qi * (tq // GRID_W) + a
        r0 = jnp.clip(rq - wh // 2, 0, rows - wh)
        for kp in range(rows // 2):
            rk = 2 * kp
            ok_lo = ((rk >= r0) & (rk < r0 + wh)).astype(jnp.int32)
            ok_hi = ((rk + 1 >= r0) & (rk + 1 < r0 + wh)).astype(jnp.int32)
            ok = jnp.where(lane < GRID_W, ok_lo, ok_hi) > 0
            idx = jnp.clip(rk - rq + NA_WIN_H, 0, 2 * NA_WIN_H - 1)
            for hh in range(2 * pp):
                tile = jnp.where(ok, tab_ref[hh, idx], NEG_INF)
                out_ref[hh // 2, hh % 2, a * GRID_W:(a + 1) * GRID_W, kp * LANES:(kp + 1) * LANES] = tile


def _attention(q, k, v, *, n_batch, seq, row0, tq, pp, q_col0=0, k_col0=0, v_col0=0, kv_div=1,
               cache_k=None, cache_v=None, bias=None, sink=None, band=False, out_prev=None, name):
    n_groups = NA_HEADS * HEAD_DIM // (LANES * pp)
    nqt = seq // tq
    qw = LANES * pp
    kw = qw // kv_div
    has_cache, has_bias, has_sink = cache_k is not None, bias is not None, sink is not None
    aliased = out_prev is not None

    in_specs = [
        pl.BlockSpec((tq, qw), lambda g, qi, b: (row0 // tq + b * nqt + qi, q_col0 + g)),
        pl.BlockSpec((seq, kw), lambda g, qi, b: (row0 // seq + b, k_col0 + g)),
        pl.BlockSpec((seq, kw), lambda g, qi, b: (row0 // seq + b, v_col0 + g)),
    ]
    args = [q, k, v]
    if has_cache:
        cspec = pl.BlockSpec((None, PAST_LEN, kw), lambda g, qi, b: (b, 0, g))
        in_specs += [cspec, cspec]
        args += [cache_k, cache_v]
    if has_bias:
        in_specs.append(pl.BlockSpec((2 * pp, 2 * NA_WIN_H, GRID_W, LANES), lambda g, qi, b: (g, 0, 0, 0)))
        args.append(bias)
    if has_sink:
        in_specs.append(pl.BlockSpec(memory_space=pltpu.SMEM))
        args.append(sink)
    if aliased:
        in_specs.append(pl.BlockSpec(memory_space=pl.ANY))
        args.append(out_prev)
    return pl.pallas_call(
        functools.partial(_attn_kernel, tq=tq, tk=seq, pp=pp, kv_div=kv_div, has_cache=has_cache,
                          has_bias=has_bias, has_sink=has_sink, band=band, aliased=aliased),
        out_shape=jax.ShapeDtypeStruct((T_ALL, NA_HEADS * HEAD_DIM), BF16),
        grid=(n_groups, nqt, n_batch),
        in_specs=in_specs,
        out_specs=pl.BlockSpec((tq, qw), lambda g, qi, b: (row0 // tq + b * nqt + qi, g)),
        input_output_aliases={len(args) - 1: 0} if aliased else {},
        scratch_shapes=[pltpu.VMEM((pp, 2, tq, seq), F32)] if has_bias else [],
        compiler_params=_params(),
        name=name,
    )(*args)


def _fourier_kernel(*refs, aliased):
    u_ref, cc_ref, sc_ref, cl_ref, sl_ref = refs[:5]
    o_ref = refs[5 + int(aliased)]
    cc_b, sc_b, cl_b, sl_b = refs[6 + int(aliased):]

    @pl.when((pl.program_id(0) == 0) & (pl.program_id(1) == 0))
    def _():
        cc_b[...] = cc_ref[...].astype(BF16)
        sc_b[...] = sc_ref[...].astype(BF16)
        cl_b[...] = cl_ref[...].astype(BF16)
        sl_b[...] = sl_ref[...].astype(BF16)

    u = u_ref[...].astype(BF16)
    a = jnp.dot(u, cc_b[...], preferred_element_type=F32).astype(BF16)
    b = jnp.dot(u, sc_b[...], preferred_element_type=F32).astype(BF16)
    o = (jnp.dot(cl_b[...], a, preferred_element_type=F32)
         + jnp.dot(sl_b[...], b, preferred_element_type=F32))
    o_ref[...] = o.astype(o_ref.dtype)


def _fourier(z, cc, sc, cl, sl, *, n_batch, seq, row0, out_prev=None, name):
    aliased = out_prev is not None
    gd = FT_GROUP_DIM
    in_specs = [
        pl.BlockSpec((seq, gd), lambda b, g: (row0 // seq + b, Z_UB // gd + g)),
        pl.BlockSpec((gd, gd), lambda b, g: (0, 0)),
        pl.BlockSpec((gd, gd), lambda b, g: (0, 0)),
        pl.BlockSpec((seq, seq), lambda b, g: (0, 0)),
        pl.BlockSpec((seq, seq), lambda b, g: (0, 0)),
    ]
    args = [z, cc, sc, cl, sl]
    if aliased:
        in_specs.append(pl.BlockSpec(memory_space=pl.ANY))
        args.append(out_prev)
    return pl.pallas_call(
        functools.partial(_fourier_kernel, aliased=aliased),
        out_shape=jax.ShapeDtypeStruct((T_ALL, FT_GROUPS * gd), BF16),
        grid=(n_batch, FT_GROUPS),
        in_specs=in_specs,
        out_specs=pl.BlockSpec((seq, gd), lambda b, g: (row0 // seq + b, g)),
        input_output_aliases={len(args) - 1: 0} if aliased else {},
        scratch_shapes=[pltpu.VMEM((gd, gd), BF16), pltpu.VMEM((gd, gd), BF16),
                        pltpu.VMEM((seq, seq), BF16), pltpu.VMEM((seq, seq), BF16)],
        compiler_params=_params(dimension_semantics=("arbitrary", "arbitrary")),
        name=name,
    )(*args)


def _merge1_kernel(oa, ob, oc, wa, wb, wc, ga, gb, gc, m_ref):
    def branch(o, w, g):
        return jax.nn.sigmoid(g[...]) * jnp.dot(o[...], w[...].astype(BF16),
                                                preferred_element_type=F32)
    m_ref[...] = (branch(oa, wa, ga) + branch(ob, wb, gb) + branch(oc, wc, gc)).astype(m_ref.dtype)


def _merge1(o_a, o_b, o_c, w_a, w_b, w_c, z, layer):
    tm, tn = 1024, 512
    kdim = NA_HEADS * HEAD_DIM
    o_spec = pl.BlockSpec((tm, kdim), lambda i, j: (i, 0))
    w_spec = pl.BlockSpec((None, kdim, tn), lambda i, j: (layer, 0, j))

    def gate_spec(br):
        return pl.BlockSpec((tm, tn), lambda i, j: (i, (Z_GATE + br * D_MODEL) // tn + j))

    return pl.pallas_call(
        _merge1_kernel,
        out_shape=jax.ShapeDtypeStruct((T_ALL, D_MODEL), BF16),
        grid=(T_ALL // tm, D_MODEL // tn),
        in_specs=[o_spec, o_spec, o_spec, w_spec, w_spec, w_spec,
                  gate_spec(0), gate_spec(1), gate_spec(2)],
        out_specs=pl.BlockSpec((tm, tn), lambda i, j: (i, j)),
        compiler_params=_params(),
        name="merge_branches",
    )(o_a, o_b, o_c, w_a, w_b, w_c, z, z, z)


def _merge2_kernel(m_ref, w_ref, x_ref, g_ref, o_ref):
    o_ref[...] = x_ref[...] + g_ref[...] * jnp.dot(m_ref[...], w_ref[...].astype(BF16),
                                                   preferred_element_type=F32)


def _merge2(m, w_o, x, mods, layer):
    tm, tn = 1024, 512
    return pl.pallas_call(
        _merge2_kernel,
        out_shape=jax.ShapeDtypeStruct((T_ALL, D_MODEL), F32),
        grid=(T_ALL // tm, D_MODEL // tn),
        in_specs=[
            pl.BlockSpec((tm, D_MODEL), lambda i, j: (i, 0)),
            pl.BlockSpec((None, D_MODEL, tn), lambda i, j: (layer, 0, j)),
            pl.BlockSpec((tm, tn), lambda i, j: (i, j)),
            pl.BlockSpec((None, 1, tn), lambda i, j: (_mod_row(i * tm), 0, 2 * D_MODEL // tn + j)),
        ],
        out_specs=pl.BlockSpec((tm, tn), lambda i, j: (i, j)),
        compiler_params=_params(),
        name="out_proj_residual",
    )(m, w_o, x, mods)


def _router_kernel(x_ref, g_ref, sh_ref, sc_ref, wr_ref, br_ref, tri_ref,
                   h_ref, e_ref, gate_ref, rank_ref, cnt_ref, carry):
    @pl.when(pl.program_id(0) == 0)
    def _():
        carry[...] = jnp.zeros_like(carry)

    h = _rms_mod(x_ref[...], g_ref[...], sh_ref[...], sc_ref[...])
    h_ref[...] = h
    tm = h.shape[0]
    logits = _dot3(wr_ref[...], h, _NT) + br_ref[...]
    eidx = lax.broadcasted_iota(jnp.int32, (N_EXPERTS, tm), 0).astype(F32)
    work = logits
    vals, hots = [], []
    for kk in range(TOP_K):
        m = jnp.max(work, axis=0, keepdims=True)
        idx = jnp.min(jnp.where(work == m, eidx, float(N_EXPERTS)), axis=0, keepdims=True)
        hot = eidx == idx
        work = jnp.where(hot, -jnp.inf, work)
        vals.append(m)
        hots.append(hot)
        e_ref[kk:kk + 1, :] = idx.astype(jnp.int32)
    ex = [jnp.exp(vv - vals[0]) for vv in vals]
    den = ex[0] + ex[1] + ex[2] + ex[3]
    for kk in range(TOP_K):
        gate_ref[kk:kk + 1, :] = ex[kk] / den
    chosen = jnp.zeros((N_EXPERTS, tm), F32)
    for hot in hots:
        chosen = chosen + jnp.where(hot, 1.0, 0.0)
    before = jnp.dot(chosen.astype(BF16), tri_ref[...], preferred_element_type=F32) + carry[...]
    for kk in range(TOP_K):
        rk = jnp.sum(jnp.where(hots[kk], before, 0.0), axis=0, keepdims=True)
        rank_ref[kk:kk + 1, :] = rk.astype(jnp.int32)
    carry[...] = carry[...] + jnp.sum(chosen, axis=1, keepdims=True)
    cnt_ref[...] = carry[...]


def _router(x, g, mods, w_router_t, b_router, tri):
    tm = ROUTER_TM
    return pl.pallas_call(
        _router_kernel,
        out_shape=(
            jax.ShapeDtypeStruct((T_ALL, D_MODEL), F32),
            jax.ShapeDtypeStruct((TOP_K, T_ALL), jnp.int32),
            jax.ShapeDtypeStruct((TOP_K, T_ALL), F32),
            jax.ShapeDtypeStruct((TOP_K, T_ALL), jnp.int32),
            jax.ShapeDtypeStruct((N_EXPERTS, 1), F32),
        ),
        grid=(T_ALL // tm,),
        in_specs=[
            pl.BlockSpec((tm, D_MODEL), lambda i: (i, 0)),
            pl.BlockSpec((1, D_MODEL), lambda i: (0, 0)),
            pl.BlockSpec((None, 1, D_MODEL), lambda i: (_mod_row(i * tm), 0, 3)),
            pl.BlockSpec((None, 1, D_MODEL), lambda i: (_mod_row(i * tm), 0, 4)),
            pl.BlockSpec((N_EXPERTS, D_MODEL), lambda i: (0, 0)),
            pl.BlockSpec((N_EXPERTS, 1), lambda i: (0, 0)),
            pl.BlockSpec((tm, tm), lambda i: (0, 0)),
        ],
        out_specs=(
            pl.BlockSpec((tm, D_MODEL), lambda i: (i, 0)),
            pl.BlockSpec((TOP_K, tm), lambda i: (0, i)),
            pl.BlockSpec((TOP_K, tm), lambda i: (0, i)),
            pl.BlockSpec((TOP_K, tm), lambda i: (0, i)),
            pl.BlockSpec((N_EXPERTS, 1), lambda i: (0, 0)),
        ),
        scratch_shapes=[pltpu.VMEM((N_EXPERTS, 1), F32)],
        compiler_params=_params(dimension_semantics=("arbitrary",)),
        name="norm2_router",
    )(x, g.reshape(1, D_MODEL), mods, mods, w_router_t, b_router.reshape(N_EXPERTS, 1), tri)


def _moe_kernel(ue_ref, urow_ref, un_ref, uf_ref, ua_ref, src_ref,
                h_hbm, wg_ref, wl_ref, bg_ref, bl_ref, wd_ref, bd_ref, out_hbm,
                xf, xb, acc, gsem, osem):
    u = pl.program_id(0)
    c = pl.program_id(1)
    nrows = un_ref[u]
    nfill = uf_ref[u]
    row0 = urow_ref[u]
    row0_next = urow_ref[jnp.minimum(u + 1, MOE_UNITS - 1)]

    def row_copy(r, tok):
        return pltpu.make_async_copy(h_hbm.at[pl.ds(tok, 1)], xf.at[pl.ds(r, 1)], gsem)

    def gather_sub(first_row, first_slot):
        for r in range(MOE_SUB):
            row_copy(first_slot + r, src_ref[first_row + r]).start()

    def out_copy(sub):
        rows = pl.ds(sub * MOE_SUB, MOE_SUB)
        dst = pl.ds(pl.multiple_of(row0 + sub * MOE_SUB, MOE_SUB), MOE_SUB)
        return pltpu.make_async_copy(acc.at[rows], out_hbm.at[dst], osem)

    @pl.when(c == 0)
    def _():
        @pl.when(u == 0)
        def _():
            for sub in range(MOE_NSUB):
                gather_sub(row0 + sub * MOE_SUB, sub * MOE_SUB)

        acc[...] = jnp.broadcast_to(bd_ref[...], acc.shape)

        @pl.when((u == 0) | (ua_ref[jnp.maximum(u - 1, 0)] > 0))
        def _():
            for r in range(MOE_UNIT):
                row_copy(r, 0).wait()

        for sub in range(MOE_NSUB):
            @pl.when(sub * MOE_SUB < nrows)
            def _():
                rows = pl.ds(sub * MOE_SUB, MOE_SUB)
                xb[rows, :] = xf[rows, :].astype(BF16)

    def expert_rows(sub, wg, wl, wd):
        rows = pl.ds(sub * MOE_SUB, MOE_SUB)
        xs = xb[rows, :]
        glu = jnp.dot(xs, wg, preferred_element_type=F32) + bg_ref[...]
        lin = jnp.dot(xs, wl, preferred_element_type=F32) + bl_ref[...]
        glu = jnp.minimum(glu, SWIGLU_LIMIT)
        lin = jnp.clip(lin, -SWIGLU_LIMIT, SWIGLU_LIMIT)
        hdn = glu * jax.nn.sigmoid(SWIGLU_ALPHA * glu) * (lin + 1.0)
        acc[rows, :] = acc[rows, :] + jnp.dot(hdn.astype(BF16), wd, preferred_element_type=F32)

    nsub = nrows // MOE_SUB
    for n in range(1, MOE_NSUB + 1):
        @pl.when(nsub == n)
        def _():
            wg = wg_ref[...].astype(BF16)
            wl = wl_ref[...].astype(BF16)
            wd = wd_ref[...].astype(BF16)
            gather_sub(row0_next + c * MOE_SUB, c * MOE_SUB)
            for sub in range(n):
                expert_rows(sub, wg, wl, wd)

    @pl.when(c == MOE_NCHUNK - 1)
    def _():
        @pl.when(nfill > 0)
        def _():
            acc[...] = jnp.zeros_like(acc)

        nout = nrows + nfill
        for sub in range(MOE_NSUB):
            @pl.when(sub * MOE_SUB < nout)
            def _():
                out_copy(sub).start()
        for sub in range(MOE_NSUB):
            @pl.when(sub * MOE_SUB < nout)
            def _():
                out_copy(sub).wait()


def _moe(h2, plan, w_gu, b_gu, w_down, b_down, layer):
    fc = MOE_FCHUNK

    def chunk(c, act, u):
        return jnp.where(act[u] > 0, c, MOE_NCHUNK - 1)

    def wg_map(u, c, ue, ur, un, uf, ua, sr):
        return (layer, ue[u], 0, chunk(c, ua, u))

    def wl_map(u, c, ue, ur, un, uf, ua, sr):
        return (layer, ue[u], 0, MOE_NCHUNK + chunk(c, ua, u))

    def wd_map(u, c, ue, ur, un, uf, ua, sr):
        return (layer, ue[u], chunk(c, ua, u), 0)

    def bd_map(u, c, ue, ur, un, uf, ua, sr):
        return (layer, ue[u], 0, 0)

    grid_spec = pltpu.PrefetchScalarGridSpec(
        num_scalar_prefetch=6,
        grid=(MOE_UNITS, MOE_NCHUNK),
        in_specs=[
            pl.BlockSpec(memory_space=pl.ANY),
            pl.BlockSpec((None, None, D_MODEL, fc), wg_map),
            pl.BlockSpec((None, None, D_MODEL, fc), wl_map),
            pl.BlockSpec((None, None, 1, fc), wg_map),
            pl.BlockSpec((None, None, 1, fc), wl_map),
            pl.BlockSpec((None, None, fc, D_MODEL), wd_map),
            pl.BlockSpec((None, None, 1, D_MODEL), bd_map),
        ],
        out_specs=pl.BlockSpec(memory_space=pl.ANY),
        scratch_shapes=[
            pltpu.VMEM((MOE_UNIT, D_MODEL), F32),
            pltpu.VMEM((MOE_UNIT, D_MODEL), BF16),
            pltpu.VMEM((MOE_UNIT, D_MODEL), F32),
            pltpu.SemaphoreType.DMA,
            pltpu.SemaphoreType.DMA,
        ],
    )
    return pl.pallas_call(
        _moe_kernel,
        out_shape=jax.ShapeDtypeStruct((MOE_ROWS, D_MODEL), F32),
        grid_spec=grid_spec,
        compiler_params=_params(dimension_semantics=("arbitrary", "arbitrary")),
        name="moe_experts",
    )(*plan, h2, w_gu, w_gu,
      b_gu.reshape(DEPTH, N_EXPERTS, 1, 2 * D_FF), b_gu.reshape(DEPTH, N_EXPERTS, 1, 2 * D_FF),
      w_down, b_down.reshape(DEPTH, N_EXPERTS, 1, D_MODEL))


def _combine_kernel(dest_ref, rows_hbm, gate_ref, x_ref, g2_ref, o_ref, buf, sem):
    tm = COMBINE_TM
    base = pl.program_id(0) * tm

    def row_copy(r, kk, d):
        return pltpu.make_async_copy(rows_hbm.at[pl.ds(d, 1)], buf.at[kk, pl.ds(r, 1)], sem)

    for r in range(tm):
        for kk in range(TOP_K):
            row_copy(r, kk, dest_ref[(base + r) * TOP_K + kk]).start()
    for r in range(tm):
        for kk in range(TOP_K):
            row_copy(r, kk, 0).wait()
    gates = gate_ref[...]
    y = gates[:, 0:1] * buf[0]
    for kk in range(1, TOP_K):
        y = y + gates[:, kk:kk + 1] * buf[kk]
    o_ref[...] = x_ref[...] + g2_ref[...] * y


def _combine(dest_flat, rows, gates_t, x, mods):
    tm = COMBINE_TM
    grid_spec = pltpu.PrefetchScalarGridSpec(
        num_scalar_prefetch=1,
        grid=(T_ALL // tm,),
        in_specs=[
            pl.BlockSpec(memory_space=pl.ANY),
            pl.BlockSpec((tm, TOP_K), lambda i, d: (i, 0)),
            pl.BlockSpec((tm, D_MODEL), lambda i, d: (i, 0)),
            pl.BlockSpec((None, 1, D_MODEL), lambda i, d: (_mod_row(i * tm), 0, 5)),
        ],
        out_specs=pl.BlockSpec((tm, D_MODEL), lambda i, d: (i, 0)),
        scratch_shapes=[pltpu.VMEM((TOP_K, tm, D_MODEL), F32), pltpu.SemaphoreType.DMA],
    )
    return pl.pallas_call(
        _combine_kernel,
        out_shape=jax.ShapeDtypeStruct((T_ALL, D_MODEL), F32),
        grid_spec=grid_spec,
        compiler_params=_params(dimension_semantics=("arbitrary",)),
        name="moe_combine",
    )(dest_flat, rows, gates_t, x, mods)


def _moe_plan(e4, r4, counts):
    cnt = counts[:, 0].astype(jnp.int32)
    pad = (cnt + MOE_SUB - 1) // MOE_SUB * MOE_SUB
    pad_end = jnp.cumsum(pad)
    pad_start = pad_end - pad
    hot = e4[:, :, None] == jnp.arange(N_EXPERTS, dtype=jnp.int32)
    dest = jnp.sum(jnp.where(hot, pad_start, 0), axis=-1) + r4
    n_units_e = (pad + MOE_UNIT - 1) // MOE_UNIT
    unit_end = jnp.cumsum(n_units_e)
    unit_start = unit_end - n_units_e
    n_units = unit_end[-1]
    uid = jnp.minimum(jnp.arange(MOE_UNITS, dtype=jnp.int32), n_units - 1)
    unit_e = jnp.minimum(jnp.sum((uid[:, None] >= unit_end[None, :]).astype(jnp.int32), axis=1),
                         N_EXPERTS - 1)
    mine = unit_e[:, None] == jnp.arange(N_EXPERTS, dtype=jnp.int32)
    pick = lambda tab: jnp.sum(jnp.where(mine, tab, 0), axis=1)
    local = uid - pick(unit_start)
    unit_row = pick(pad_start) + local * MOE_UNIT
    unit_act = (jnp.arange(MOE_UNITS, dtype=jnp.int32) < n_units).astype(jnp.int32)
    unit_n = jnp.clip(pick(pad) - local * MOE_UNIT, 0, MOE_UNIT) * unit_act
    fill_row = pad_end[-1] + (jnp.arange(MOE_UNITS, dtype=jnp.int32) - n_units) * MOE_UNIT
    unit_fill = jnp.clip(MOE_ROWS - fill_row, 0, MOE_UNIT) * (1 - unit_act)
    unit_row = jnp.where(unit_act > 0, unit_row, jnp.minimum(fill_row, MOE_ROWS - MOE_SUB))
    tok = jnp.broadcast_to(jnp.arange(T_ALL, dtype=jnp.int32)[None, :], (TOP_K, T_ALL))
    src = jnp.zeros((MOE_ROWS + MOE_UNIT,), jnp.int32).at[dest.reshape(-1)].set(tok.reshape(-1))
    i32 = lambda a: a.astype(jnp.int32)
    return dest, (unit_e, i32(unit_row), i32(unit_n), i32(unit_fill), unit_act, src)


def _dft_tables(n):
    idx = np.outer(np.arange(n), np.arange(n)) % n
    ang = 2.0 * np.pi * idx / n
    s = 1.0 / np.sqrt(n)
    return np.cos(ang) * s, np.sin(ang) * s


def _rope_tables():
    t = np.arange(LAT_SEQ)
    nf = HEAD_DIM // 4
    inv = ROPE_BASE ** (-np.arange(nf, dtype=np.float64) / nf)
    ang = np.concatenate([(t // GRID_W)[:, None] * inv, (t % GRID_W)[:, None] * inv], axis=-1)
    lane = np.arange(LANES)
    cos = np.cos(ang)[:, lane % (HEAD_DIM // 2)]
    sin = np.sin(ang)[:, lane % (HEAD_DIM // 2)]
    sign = np.where((lane % HEAD_DIM) < HEAD_DIM // 2, -1.0, 1.0)
    return cos.astype(np.float32), (sin * sign).astype(np.float32)


def _na_bias(rpb):
    c = np.arange(GRID_W)
    c0 = np.clip(c - NA_WIN_W // 2, 0, GRID_W - NA_WIN_W)
    c_ok = (c[None, :] >= c0[:, None]) & (c[None, :] < c0[:, None] + NA_WIN_W)
    dc = c[None, :] - c[:, None] + NA_WIN_W - 1
    pick = (np.arange(2 * NA_WIN_W - 1)[:, None, None] == dc[None]) & c_ok[None]
    cols = jnp.einsum('hrd,dqk->hrqk', rpb.astype(F32), jnp.asarray(pick, F32),
                      precision=lax.Precision.HIGHEST)
    cols = jnp.where(jnp.asarray(c_ok)[None, None], cols, NEG_INF)
    edge = jnp.full((NA_HEADS, 1, GRID_W, GRID_W), NEG_INF, F32)
    return jnp.concatenate([jnp.concatenate([edge, cols], axis=1),
                            jnp.concatenate([cols, edge], axis=1)], axis=-1)


def _dup_heads(cache):
    b, p = cache.shape[:2]
    return jnp.broadcast_to(cache[:, :, :, None, :], (b, p, SW_KV_HEADS, 2, HEAD_DIM)).reshape(
        b, p, SW_KV_HEADS * LANES)


def kernel(x_prompt, x_sample, cache_na_k, cache_na_v, cache_sw_k, cache_sw_v, c, c_ctx,
           w_ada, b_ada, norm1_g, norm2_g, w_in, qn_na, kn_na, qn_sw, kn_sw, rpb_na, sink_sw,
           w_br_na, w_br_ft, w_br_sw, w_o, w_router, b_router, w_gu, b_gu, w_down, b_down):
    nq = NA_HEADS * HEAD_DIM
    x = jnp.concatenate([x_prompt.reshape(T_CTX, D_MODEL), x_sample.reshape(T_LAT, D_MODEL)], axis=0)
    cvecs = jnp.zeros((MOD_ROWS, D_MODEL), F32).at[:LAT_BATCH].set(c).at[CTX_MOD_ROW].set(c_ctx)
    mods_all = _ada(cvecs, w_ada, b_ada)

    cc, sc = _dft_tables(FT_GROUP_DIM)
    cl_ctx, sl_ctx = _dft_tables(CTX_SEQ)
    cl_lat, sl_lat = _dft_tables(LAT_SEQ)
    as_bf16 = lambda a: jnp.asarray(a, BF16)
    cos_t, sin_t = _rope_tables()
    bd = as_bf16(np.kron(np.eye(LANES // HEAD_DIM), np.ones((HEAD_DIM, HEAD_DIM))))
    tri = as_bf16(np.triu(np.ones((ROUTER_TM, ROUTER_TM)), 1))
    tile2 = lambda g: jnp.tile(g.reshape(1, HEAD_DIM), (1, LANES // HEAD_DIM))

    na_k, na_v, sw_k, sw_v = [], [], [], []
    for l in range(DEPTH):
        mods = mods_all[l].reshape(MOD_ROWS, 1, N_MOD)
        h = _prenorm(x, norm1_g[l], mods)
        z = _in_proj(h, w_in, l)
        qa, ka, qc, kc_n, kc_dup, vc_dup = _prep(z, tile2(qn_na[l]), tile2(kn_na[l]), tile2(qn_sw[l]),
                                                 tile2(kn_sw[l]), bd, jnp.asarray(cos_t),
                                                 jnp.asarray(sin_t))
        na_k.append(ka[:T_CTX].reshape(CTX_BATCH, CTX_SEQ, NA_HEADS, HEAD_DIM))
        na_v.append(z[:T_CTX, Z_VA:Z_VA + nq].reshape(CTX_BATCH, CTX_SEQ, NA_HEADS, HEAD_DIM))
        sw_k.append(kc_n[:T_CTX].reshape(CTX_BATCH, CTX_SEQ, SW_KV_HEADS, HEAD_DIM))
        sw_v.append(z[:T_CTX, Z_VC:Z_VC + SW_KV_HEADS * HEAD_DIM].reshape(
            CTX_BATCH, CTX_SEQ, SW_KV_HEADS, HEAD_DIM))

        ctx = dict(n_batch=CTX_BATCH, seq=CTX_SEQ, row0=0, tq=CTX_SEQ, pp=4)
        lat = dict(n_batch=LAT_BATCH, seq=LAT_SEQ, row0=T_CTX, tq=256, pp=2)
        o_init = jnp.zeros((T_ALL, nq), BF16)
        o_a = _attention(qa, ka, z, v_col0=Z_VA // (LANES * ctx['pp']), out_prev=o_init,
                         name="attn_na_ctx", **ctx)
        o_a = _attention(qa, ka, z, v_col0=Z_VA // (LANES * lat['pp']),
                         cache_k=cache_na_k[:, l].reshape(LAT_BATCH, PAST_LEN, nq),
                         cache_v=cache_na_v[:, l].reshape(LAT_BATCH, PAST_LEN, nq),
                         bias=_na_bias(rpb_na[l]), out_prev=o_a, name="attn_na_lat", **lat)
        o_c = _attention(qc, kc_dup, vc_dup, kv_div=2, sink=sink_sw[l], out_prev=o_init,
                         name="attn_sw_ctx", **ctx)
        o_c = _attention(qc, kc_dup, vc_dup, kv_div=2, sink=sink_sw[l],
                         cache_k=_dup_heads(cache_sw_k[:, l]), cache_v=_dup_heads(cache_sw_v[:, l]),
                         band=True, out_prev=o_c, name="attn_sw_lat", **lat)
        as_f32 = lambda a: jnp.asarray(a, F32)
        o_b = _fourier(z, as_f32(cc), as_f32(-sc), as_f32(cl_ctx), as_f32(sl_ctx),
                       n_batch=CTX_BATCH, seq=CTX_SEQ, row0=0, out_prev=o_init, name="fourier_ctx")
        o_b = _fourier(z, as_f32(cc), as_f32(-sc), as_f32(cl_lat), as_f32(sl_lat),
                       n_batch=LAT_BATCH, seq=LAT_SEQ, row0=T_CTX, out_prev=o_b, name="fourier_lat")
        m = _merge1(o_a, o_b, o_c, w_br_na, w_br_ft, w_br_sw, z, l)
        x = _merge2(m, w_o, x, mods, l)

        h2, e4, g4, r4, counts = _router(x, norm2_g[l], mods, w_router[l].T, b_router[l], tri)
        dest, plan = _moe_plan(e4, r4, counts)
        rows = _moe(h2, plan, w_gu, b_gu, w_down, b_down, l)
        x = _combine(dest.T.reshape(-1), rows, g4.T, x, mods)

    y_prompt = x[:T_CTX].reshape(CTX_BATCH, CTX_SEQ, D_MODEL)
    y_sample = x[T_CTX:].reshape(LAT_BATCH, LAT_SEQ, D_MODEL)
    return (y_prompt, y_sample, jnp.stack(na_k, axis=1), jnp.stack(na_v, axis=1),
            jnp.stack(sw_k, axis=1), jnp.stack(sw_v, axis=1))
```

```python
import functools

import numpy as np
import jax
import jax.numpy as jnp
from jax import lax
from jax.experimental import pallas as pl
from jax.experimental.pallas import tpu as pltpu

F32 = jnp.float32
BF16 = jnp.bfloat16

D_MODEL = 2048
DEPTH = 2
CTX_BATCH = 32
CTX_SEQ = 256
LAT_BATCH = 4
LAT_SEQ = 1024
PAST_LEN = 256
T_CTX = CTX_BATCH * CTX_SEQ
T_LAT = LAT_BATCH * LAT_SEQ
T_ALL = T_CTX + T_LAT
GRID_W = 64
HEAD_DIM = 64
NA_HEADS = 16
NA_WIN_H = 8
NA_WIN_W = 16
FT_GROUPS = 4
FT_GROUP_DIM = 256
SW_HEADS = 16
SW_KV_HEADS = 4
SW_WINDOW = 128
ROPE_BASE = 10000.0
N_EXPERTS = 32
TOP_K = 4
D_FF = 2048
SWIGLU_LIMIT = 7.0
SWIGLU_ALPHA = 1.702
EPS = 1e-6
NEG_INF = -1e30
NA_KEY_ROWS = 12

Z_QA, Z_KA, Z_VA, Z_UB, Z_QC, Z_KC, Z_VC, Z_GATE = 0, 1024, 2048, 3072, 4096, 5120, 5376, 5632
IN_WIDTH = Z_GATE + 3 * D_MODEL
N_MOD = 6 * D_MODEL
CTX_MOD_ROW = LAT_BATCH
MOD_ROWS = 8

LANES = 128
VMEM_LIMIT = 56 * 1024 * 1024

MOE_SUB = 256
MOE_UNIT = 1024
MOE_NSUB = MOE_UNIT // MOE_SUB
MOE_FCHUNK = 512
MOE_NCHUNK = D_FF // MOE_FCHUNK
N_ASSIGN = T_ALL * TOP_K
MOE_ROWS = -(-(N_ASSIGN + N_EXPERTS * (MOE_SUB - 1)) // MOE_UNIT) * MOE_UNIT
MOE_UNITS = MOE_ROWS // MOE_UNIT + N_EXPERTS
MOE_SRC_LEN = MOE_ROWS + MOE_UNIT
ROUTER_TM = 256
COMBINE_TM = 256


def _params(**kw):
    return pltpu.CompilerParams(vmem_limit_bytes=VMEM_LIMIT, **kw)


def _mod_row(row0):
    return jnp.where(row0 < T_CTX, CTX_MOD_ROW, (row0 - T_CTX) // LAT_SEQ)


def _split(a):
    hi = a.astype(BF16)
    lo = (a - hi.astype(F32)).astype(BF16)
    return hi, lo


def _dot3(a, b, dims):
    ah, al = _split(a)
    bh, bl = _split(b)
    d = functools.partial(lax.dot_general, dimension_numbers=dims, preferred_element_type=F32)
    return d(ah, bh) + d(ah, bl) + d(al, bh)


_NN = (((1,), (0,)), ((), ()))
_NT = (((1,), (1,)), ((), ()))


def _ada_kernel(c_ref, w_ref, b_ref, o_ref):
    c = c_ref[...]
    s = c * jax.nn.sigmoid(c)
    o_ref[...] = _dot3(s, w_ref[...], _NN) + b_ref[...]


def _ada(cvecs, w_ada, b_ada):
    tn = 1024
    return pl.pallas_call(
        _ada_kernel,
        out_shape=jax.ShapeDtypeStruct((DEPTH, MOD_ROWS, N_MOD), F32),
        grid=(DEPTH, N_MOD // tn),
        in_specs=[
            pl.BlockSpec((MOD_ROWS, D_MODEL), lambda l, j: (0, 0)),
            pl.BlockSpec((None, D_MODEL, tn), lambda l, j: (l, 0, j)),
            pl.BlockSpec((None, 1, tn), lambda l, j: (l, 0, j)),
        ],
        out_specs=pl.BlockSpec((None, MOD_ROWS, tn), lambda l, j: (l, 0, j)),
        compiler_params=_params(),
        name="ada_table",
    )(cvecs, w_ada, b_ada.reshape(DEPTH, 1, N_MOD))


def _rms_mod(x, g, shift, scale):
    y = x * lax.rsqrt(jnp.mean(x * x, axis=-1, keepdims=True) + EPS)
    return (y * g) * (1.0 + scale) + shift


def _token_specs(x, tm, tn):
    col = lambda j: j[0] if j else 0
    if not isinstance(x, tuple):
        return [pl.BlockSpec((tm, tn), lambda i, *j: (i, col(j)))], [x]
    n_ctx = T_CTX // tm
    return ([pl.BlockSpec((tm, tn), lambda i, *j: (jnp.minimum(i, n_ctx - 1),
                                                   jnp.where(i < n_ctx, col(j), 0))),
             pl.BlockSpec((tm, tn), lambda i, *j: (jnp.maximum(i - n_ctx, 0),
                                                   jnp.where(i >= n_ctx, col(j), 0)))], list(x))


def _token_tile(refs, tm):
    if len(refs) == 1:
        return refs[0][...]
    return jnp.where(pl.program_id(0) * tm >= T_CTX, refs[1][...], refs[0][...])


def _prenorm_kernel(*refs, tm):
    g_ref, sh_ref, sc_ref, h_ref = refs[-4:]
    x = _token_tile(refs[:-4], tm)
    h_ref[...] = _rms_mod(x, g_ref[...], sh_ref[...], sc_ref[...]).astype(h_ref.dtype)


def _prenorm(x, g, mods):
    tm = 512
    x_specs, x_args = _token_specs(x, tm, D_MODEL)
    return pl.pallas_call(
        functools.partial(_prenorm_kernel, tm=tm),
        out_shape=jax.ShapeDtypeStruct((T_ALL, D_MODEL), BF16),
        grid=(T_ALL // tm,),
        in_specs=x_specs + [
            pl.BlockSpec((1, D_MODEL), lambda i: (0, 0)),
            pl.BlockSpec((None, 1, D_MODEL), lambda i: (_mod_row(i * tm), 0, 0)),
            pl.BlockSpec((None, 1, D_MODEL), lambda i: (_mod_row(i * tm), 0, 1)),
        ],
        out_specs=pl.BlockSpec((tm, D_MODEL), lambda i: (i, 0)),
        compiler_params=_params(),
        name="prenorm",
    )(*x_args, g.reshape(1, D_MODEL), mods, mods)


def _mm_kernel(a_ref, w_ref, o_ref):
    o_ref[...] = jnp.dot(a_ref[...], w_ref[...].astype(BF16),
                         preferred_element_type=F32).astype(o_ref.dtype)


def _in_proj(h, w_in, layer):
    tm, tn = 2048, 512
    return pl.pallas_call(
        _mm_kernel,
        out_shape=jax.ShapeDtypeStruct((T_ALL, IN_WIDTH), F32),
        grid=(T_ALL // tm, IN_WIDTH // tn),
        in_specs=[
            pl.BlockSpec((tm, D_MODEL), lambda i, j: (i, 0)),
            pl.BlockSpec((None, D_MODEL, tn), lambda i, j: (layer, 0, j)),
        ],
        out_specs=pl.BlockSpec((tm, tn), lambda i, j: (i, j)),
        compiler_params=_params(),
        name="in_proj",
    )(h, w_in)


def _head_norm(x, gain, bd):
    hi, lo = _split(x * x)
    ss = jnp.dot(hi, bd, preferred_element_type=F32) + jnp.dot(lo, bd, preferred_element_type=F32)
    return x * lax.rsqrt(ss * (1.0 / HEAD_DIM) + EPS) * gain


def _prep_kernel(qa_ref, ka_ref, qc_ref, kc_ref, vc_ref, gqa_ref, gka_ref, gqc_ref, gkc_ref,
                 bd_ref, cos_ref, sin_ref,
                 qa_o, ka_o, qc_o, kcn_o, kcd_o, vcd_o, *, tm):
    is_lat = pl.program_id(0) * tm >= T_CTX
    lane = lax.broadcasted_iota(jnp.int32, (1, LANES), 1)
    first_half = (lane % HEAD_DIM) < (HEAD_DIM // 2)
    low_head = lane < HEAD_DIM
    bd = bd_ref[...]
    cos = jnp.where(is_lat, cos_ref[...], 1.0)
    sin = jnp.where(is_lat, sin_ref[...], 0.0)

    def rope(y):
        partner = jnp.where(first_half, pltpu.roll(y, LANES - HEAD_DIM // 2, 1),
                            pltpu.roll(y, HEAD_DIM // 2, 1))
        return y * cos + partner * sin

    for s in range(NA_HEADS * HEAD_DIM // LANES):
        sl = slice(s * LANES, (s + 1) * LANES)
        qa_o[:, sl] = _head_norm(qa_ref[:, sl], gqa_ref[...], bd)
        ka_o[:, sl] = _head_norm(ka_ref[:, sl], gka_ref[...], bd)
        qc_o[:, sl] = rope(_head_norm(qc_ref[:, sl], gqc_ref[...], bd))
    for s in range(SW_KV_HEADS * HEAD_DIM // LANES):
        sl = slice(s * LANES, (s + 1) * LANES)
        kn = _head_norm(kc_ref[:, sl], gkc_ref[...], bd)
        kcn_o[:, sl] = kn
        kr = rope(kn)
        v = vc_ref[:, sl]
        kr_sw = pltpu.roll(kr, HEAD_DIM, 1)
        v_sw = pltpu.roll(v, HEAD_DIM, 1)
        kcd_o[:, (2 * s) * LANES:(2 * s + 1) * LANES] = jnp.where(low_head, kr, kr_sw)
        kcd_o[:, (2 * s + 1) * LANES:(2 * s + 2) * LANES] = jnp.where(low_head, kr_sw, kr)
        vcd_o[:, (2 * s) * LANES:(2 * s + 1) * LANES] = jnp.where(low_head, v, v_sw)
        vcd_o[:, (2 * s + 1) * LANES:(2 * s + 2) * LANES] = jnp.where(low_head, v_sw, v)


def _prep(z, gqa, gka, gqc, gkc, bd, cos_t, sin_t):
    tm = 256
    nq = NA_HEADS * HEAD_DIM
    nkv = SW_KV_HEADS * HEAD_DIM
    lat_tiles = LAT_SEQ // tm

    def pos_map(i):
        return (jnp.maximum(i - T_CTX // tm, 0) % lat_tiles, 0)

    gain = pl.BlockSpec((1, LANES), lambda i: (0, 0))
    return pl.pallas_call(
        functools.partial(_prep_kernel, tm=tm),
        out_shape=(
            jax.ShapeDtypeStruct((T_ALL, nq), F32),
            jax.ShapeDtypeStruct((T_ALL, nq), F32),
            jax.ShapeDtypeStruct((T_ALL, nq), F32),
            jax.ShapeDtypeStruct((T_ALL, nkv), F32),
            jax.ShapeDtypeStruct((T_ALL, 2 * nkv), F32),
            jax.ShapeDtypeStruct((T_ALL, 2 * nkv), F32),
        ),
        grid=(T_ALL // tm,),
        in_specs=[
            pl.BlockSpec((tm, nq), lambda i: (i, Z_QA // nq)),
            pl.BlockSpec((tm, nq), lambda i: (i, Z_KA // nq)),
            pl.BlockSpec((tm, nq), lambda i: (i, Z_QC // nq)),
            pl.BlockSpec((tm, nkv), lambda i: (i, Z_KC // nkv)),
            pl.BlockSpec((tm, nkv), lambda i: (i, Z_VC // nkv)),
            gain, gain, gain, gain,
            pl.BlockSpec((LANES, LANES), lambda i: (0, 0)),
            pl.BlockSpec((tm, LANES), pos_map),
            pl.BlockSpec((tm, LANES), pos_map),
        ],
        out_specs=(
            pl.BlockSpec((tm, nq), lambda i: (i, 0)),
            pl.BlockSpec((tm, nq), lambda i: (i, 0)),
            pl.BlockSpec((tm, nq), lambda i: (i, 0)),
            pl.BlockSpec((tm, nkv), lambda i: (i, 0)),
            pl.BlockSpec((tm, 2 * nkv), lambda i: (i, 0)),
            pl.BlockSpec((tm, 2 * nkv), lambda i: (i, 0)),
        ),
        compiler_params=_params(),
        name="qk_prep",
    )(z, z, z, z, z, gqa, gka, gqc, gkc, bd, cos_t, sin_t)


def _attn_kernel(*refs, tq, tk, kwin, pp, kv_div, has_cache, has_bias, has_sink, band, aliased):
    it = iter(refs)
    q_ref, k_ref, v_ref = next(it), next(it), next(it)
    kc_ref = vc_ref = bias_ref = sink_ref = None
    if has_cache:
        kc_ref, vc_ref = next(it), next(it)
    if has_bias:
        bias_ref = next(it)
    if has_sink:
        sink_ref = next(it)
    if aliased:
        next(it)
    o_ref = next(it)
    bias_scr = next(it) if has_bias else None

    group = pl.program_id(0)
    qi = pl.program_id(1)
    lane = lax.broadcasted_iota(jnp.int32, (1, LANES), 1)
    krows = slice(None)
    if band:
        kstart = pl.multiple_of(jnp.clip(qi * tq - SW_WINDOW, 0, tk - kwin), LANES)
        krows = pl.ds(kstart, kwin)
        qpos = qi * tq + lax.broadcasted_iota(jnp.int32, (tq, 1), 0)
        kpos = kstart + lax.broadcasted_iota(jnp.int32, (1, kwin), 1)
        in_band = jnp.abs(qpos - kpos) <= SW_WINDOW
    if has_bias:
        first_row = jnp.clip(qi * (tq // GRID_W) - NA_WIN_H // 2, 0, (tk - kwin) // GRID_W)
        krows = pl.ds(pl.multiple_of(first_row * GRID_W, LANES), kwin)

        @pl.when(pl.program_id(2) == 0)
        def _():
            _na_bias_tile(bias_ref, bias_scr, qi, tq, pp, lane, first_row, kwin)

    for j in range(pp):
        qs = slice(j * LANES, (j + 1) * LANES)
        ks = slice((j // kv_div) * LANES, (j // kv_div + 1) * LANES)
        q = q_ref[:, qs] * (HEAD_DIM ** -0.5)
        k = k_ref[krows, ks].astype(BF16)
        v = v_ref[krows, ks]
        if has_cache:
            kc = kc_ref[:, ks].astype(BF16)
            vc = vc_ref[:, ks]
        acc = jnp.zeros((tq, LANES), F32)
        for half in range(2):
            hm = (lane < HEAD_DIM) if half == 0 else (lane >= HEAD_DIM)
            qm = jnp.where(hm, q, 0.0).astype(BF16)
            s = lax.dot_general(qm, k, _NT, preferred_element_type=F32)
            if has_bias:
                s = s + bias_scr[j, half]
            if band:
                s = jnp.where(in_band, s, NEG_INF)
            m = jnp.max(s, axis=1, keepdims=True)
            if has_cache:
                sc = lax.dot_general(qm, kc, _NT, preferred_element_type=F32)
                m = jnp.maximum(m, jnp.max(sc, axis=1, keepdims=True))
            if has_sink:
                sk = sink_ref[2 * (group * pp + j) + half]
                m = jnp.maximum(m, sk)
            p = jnp.exp(s - m)
            denom = jnp.sum(p, axis=1, keepdims=True)
            o = jnp.dot(p.astype(BF16), jnp.where(hm, v, 0.0).astype(BF16),
                        preferred_element_type=F32)
            if has_cache:
                pc = jnp.exp(sc - m)
                denom = denom + jnp.sum(pc, axis=1, keepdims=True)
                o = o + jnp.dot(pc.astype(BF16), jnp.where(hm, vc, 0.0).astype(BF16),
                                preferred_element_type=F32)
            if has_sink:
                denom = denom + jnp.exp(sk - m)
            acc = acc + o / denom
        o_ref[:, qs] = acc.astype(o_ref.dtype)


def _na_bias_tile(tab_ref, out_ref, qi, tq, pp, lane, first_row, kwin):
    rows = LAT_SEQ // GRID_W
    wh = min(NA_WIN_H, rows)
    for a in range(tq // GRID_W):
        rq = qi * (tq // GRID_W) + a
        r0 = jnp.clip(rq - wh // 2, 0, rows - wh)
        for kp in range(kwin // LANES):
            rk = first_row + 2 * kp
            ok_lo = ((rk >= r0) & (rk < r0 + wh)).astype(jnp.int32)
            ok_hi = ((rk + 1 >= r0) & (rk + 1 < r0 + wh)).astype(jnp.int32)
            ok = jnp.where(lane < GRID_W, ok_lo, ok_hi) > 0
            idx = jnp.clip(rk - rq + NA_WIN_H, 0, 2 * NA_WIN_H - 1)
            for hh in range(2 * pp):
                tile = jnp.where(ok, tab_ref[hh, idx], NEG_INF)
                out_ref[hh // 2, hh % 2, a * GRID_W:(a + 1) * GRID_W, kp * LANES:(kp + 1) * LANES] = tile


def _attention(q, k, v, *, n_batch, seq, row0, tq, pp, q_col0=0, k_col0=0, v_col0=0, kv_div=1,
               cache_k=None, cache_v=None, bias=None, sink=None, band=False, kwin=None,
               out_prev=None, name):
    kwin = seq if kwin is None else kwin
    n_groups = NA_HEADS * HEAD_DIM // (LANES * pp)
    nqt = seq // tq
    qw = LANES * pp
    kw = qw // kv_div
    has_cache, has_bias, has_sink = cache_k is not None, bias is not None, sink is not None
    aliased = out_prev is not None

    in_specs = [
        pl.BlockSpec((tq, qw), lambda g, qi, b: (row0 // tq + b * nqt + qi, q_col0 + g)),
        pl.BlockSpec((seq, kw), lambda g, qi, b: (row0 // seq + b, k_col0 + g)),
        pl.BlockSpec((seq, kw), lambda g, qi, b: (row0 // seq + b, v_col0 + g)),
    ]
    args = [q, k, v]
    if has_cache:
        cspec = pl.BlockSpec((None, PAST_LEN, kw), lambda g, qi, b: (b, 0, g))
        in_specs += [cspec, cspec]
        args += [cache_k, cache_v]
    if has_bias:
        in_specs.append(pl.BlockSpec((2 * pp, 2 * NA_WIN_H, GRID_W, LANES), lambda g, qi, b: (g, 0, 0, 0)))
        args.append(bias)
    if has_sink:
        in_specs.append(pl.BlockSpec(memory_space=pltpu.SMEM))
        args.append(sink)
    if aliased:
        in_specs.append(pl.BlockSpec(memory_space=pl.ANY))
        args.append(out_prev)
    return pl.pallas_call(
        functools.partial(_attn_kernel, tq=tq, tk=seq, kwin=kwin, pp=pp, kv_div=kv_div, has_cache=has_cache,
                          has_bias=has_bias, has_sink=has_sink, band=band, aliased=aliased),
        out_shape=jax.ShapeDtypeStruct((T_ALL, NA_HEADS * HEAD_DIM), BF16),
        grid=(n_groups, nqt, n_batch),
        in_specs=in_specs,
        out_specs=pl.BlockSpec((tq, qw), lambda g, qi, b: (row0 // tq + b * nqt + qi, g)),
        input_output_aliases={len(args) - 1: 0} if aliased else {},
        scratch_shapes=[pltpu.VMEM((pp, 2, tq, kwin), F32)] if has_bias else [],
        compiler_params=_params(),
        name=name,
    )(*args)


def _fourier_kernel(*refs, aliased):
    u_ref, cc_ref, sc_ref, cl_ref, sl_ref = refs[:5]
    o_ref = refs[5 + int(aliased)]
    cc_b, sc_b, cl_b, sl_b = refs[6 + int(aliased):]

    @pl.when((pl.program_id(0) == 0) & (pl.program_id(1) == 0))
    def _():
        cc_b[...] = cc_ref[...].astype(BF16)
        sc_b[...] = sc_ref[...].astype(BF16)
        cl_b[...] = cl_ref[...].astype(BF16)
        sl_b[...] = sl_ref[...].astype(BF16)

    u = u_ref[...].astype(BF16)
    a = jnp.dot(u, cc_b[...], preferred_element_type=F32).astype(BF16)
    b = jnp.dot(u, sc_b[...], preferred_element_type=F32).astype(BF16)
    o = (jnp.dot(cl_b[...], a, preferred_element_type=F32)
         + jnp.dot(sl_b[...], b, preferred_element_type=F32))
    o_ref[...] = o.astype(o_ref.dtype)


def _fourier(z, cc, sc, cl, sl, *, n_batch, seq, row0, out_prev=None, name):
    aliased = out_prev is not None
    gd = FT_GROUP_DIM
    in_specs = [
        pl.BlockSpec((seq, gd), lambda b, g: (row0 // seq + b, Z_UB // gd + g)),
        pl.BlockSpec((gd, gd), lambda b, g: (0, 0)),
        pl.BlockSpec((gd, gd), lambda b, g: (0, 0)),
        pl.BlockSpec((seq, seq), lambda b, g: (0, 0)),
        pl.BlockSpec((seq, seq), lambda b, g: (0, 0)),
    ]
    args = [z, cc, sc, cl, sl]
    if aliased:
        in_specs.append(pl.BlockSpec(memory_space=pl.ANY))
        args.append(out_prev)
    return pl.pallas_call(
        functools.partial(_fourier_kernel, aliased=aliased),
        out_shape=jax.ShapeDtypeStruct((T_ALL, FT_GROUPS * gd), BF16),
        grid=(n_batch, FT_GROUPS),
        in_specs=in_specs,
        out_specs=pl.BlockSpec((seq, gd), lambda b, g: (row0 // seq + b, g)),
        input_output_aliases={len(args) - 1: 0} if aliased else {},
        scratch_shapes=[pltpu.VMEM((gd, gd), BF16), pltpu.VMEM((gd, gd), BF16),
                        pltpu.VMEM((seq, seq), BF16), pltpu.VMEM((seq, seq), BF16)],
        compiler_params=_params(dimension_semantics=("arbitrary", "arbitrary")),
        name=name,
    )(*args)


def _merge1_kernel(oa, ob, oc, wa, wb, wc, ga, gb, gc, m_ref):
    def branch(o, w, g):
        return jax.nn.sigmoid(g[...]) * jnp.dot(o[...], w[...].astype(BF16),
                                                preferred_element_type=F32)
    m_ref[...] = (branch(oa, wa, ga) + branch(ob, wb, gb) + branch(oc, wc, gc)).astype(m_ref.dtype)


def _merge1(o_a, o_b, o_c, w_a, w_b, w_c, z, layer):
    tm, tn = 1024, 512
    kdim = NA_HEADS * HEAD_DIM
    o_spec = pl.BlockSpec((tm, kdim), lambda i, j: (i, 0))
    w_spec = pl.BlockSpec((None, kdim, tn), lambda i, j: (layer, 0, j))

    def gate_spec(br):
        return pl.BlockSpec((tm, tn), lambda i, j: (i, (Z_GATE + br * D_MODEL) // tn + j))

    return pl.pallas_call(
        _merge1_kernel,
        out_shape=jax.ShapeDtypeStruct((T_ALL, D_MODEL), BF16),
        grid=(T_ALL // tm, D_MODEL // tn),
        in_specs=[o_spec, o_spec, o_spec, w_spec, w_spec, w_spec,
                  gate_spec(0), gate_spec(1), gate_spec(2)],
        out_specs=pl.BlockSpec((tm, tn), lambda i, j: (i, j)),
        compiler_params=_params(),
        name="merge_branches",
    )(o_a, o_b, o_c, w_a, w_b, w_c, z, z, z)


def _merge2_kernel(m_ref, w_ref, g_ref, *refs, tm):
    o_ref = refs[-1]
    x = _token_tile(refs[:-1], tm)
    o_ref[...] = x + g_ref[...] * jnp.dot(m_ref[...], w_ref[...].astype(BF16),
                                          preferred_element_type=F32)


def _merge2(m, w_o, x, mods, layer):
    tm, tn = 1024, 512
    x_specs, x_args = _token_specs(x, tm, tn)
    return pl.pallas_call(
        functools.partial(_merge2_kernel, tm=tm),
        out_shape=jax.ShapeDtypeStruct((T_ALL, D_MODEL), F32),
        grid=(T_ALL // tm, D_MODEL // tn),
        in_specs=[
            pl.BlockSpec((tm, D_MODEL), lambda i, j: (i, 0)),
            pl.BlockSpec((None, D_MODEL, tn), lambda i, j: (layer, 0, j)),
            pl.BlockSpec((None, 1, tn), lambda i, j: (_mod_row(i * tm), 0, 2 * D_MODEL // tn + j)),
        ] + x_specs,
        out_specs=pl.BlockSpec((tm, tn), lambda i, j: (i, j)),
        compiler_params=_params(),
        name="out_proj_residual",
    )(m, w_o, mods, *x_args)


def _router_kernel(x_ref, g_ref, sh_ref, sc_ref, wr_ref, br_ref, tri_ref,
                   h_ref, e_ref, gate_ref, rank_ref, cnt_ref, carry):
    @pl.when(pl.program_id(0) == 0)
    def _():
        carry[...] = jnp.zeros_like(carry)

    h = _rms_mod(x_ref[...], g_ref[...], sh_ref[...], sc_ref[...])
    h_ref[...] = h
    tm = h.shape[0]
    logits = _dot3(wr_ref[...], h, _NT) + br_ref[...]
    eidx = lax.broadcasted_iota(jnp.int32, (N_EXPERTS, tm), 0).astype(F32)
    work = logits
    vals, hots = [], []
    for kk in range(TOP_K):
        m = jnp.max(work, axis=0, keepdims=True)
        idx = jnp.min(jnp.where(work == m, eidx, float(N_EXPERTS)), axis=0, keepdims=True)
        hot = eidx == idx
        work = jnp.where(hot, -jnp.inf, work)
        vals.append(m)
        hots.append(hot)
        e_ref[kk:kk + 1, :] = idx.astype(jnp.int32)
    ex = [jnp.exp(vv - vals[0]) for vv in vals]
    den = ex[0] + ex[1] + ex[2] + ex[3]
    for kk in range(TOP_K):
        gate_ref[kk:kk + 1, :] = ex[kk] / den
    chosen = jnp.zeros((N_EXPERTS, tm), F32)
    for hot in hots:
        chosen = chosen + jnp.where(hot, 1.0, 0.0)
    before = jnp.dot(chosen.astype(BF16), tri_ref[...], preferred_element_type=F32) + carry[...]
    for kk in range(TOP_K):
        rk = jnp.sum(jnp.where(hots[kk], before, 0.0), axis=0, keepdims=True)
        rank_ref[kk:kk + 1, :] = rk.astype(jnp.int32)
    carry[...] = carry[...] + jnp.sum(chosen, axis=1, keepdims=True)
    cnt_ref[...] = carry[...]


def _router(x, g, mods, w_router_t, b_router, tri):
    tm = ROUTER_TM
    return pl.pallas_call(
        _router_kernel,
        out_shape=(
            jax.ShapeDtypeStruct((T_ALL, D_MODEL), F32),
            jax.ShapeDtypeStruct((TOP_K, T_ALL), jnp.int32),
            jax.ShapeDtypeStruct((TOP_K, T_ALL), F32),
            jax.ShapeDtypeStruct((TOP_K, T_ALL), jnp.int32),
            jax.ShapeDtypeStruct((N_EXPERTS, 1), F32),
        ),
        grid=(T_ALL // tm,),
        in_specs=[
            pl.BlockSpec((tm, D_MODEL), lambda i: (i, 0)),
            pl.BlockSpec((1, D_MODEL), lambda i: (0, 0)),
            pl.BlockSpec((None, 1, D_MODEL), lambda i: (_mod_row(i * tm), 0, 3)),
            pl.BlockSpec((None, 1, D_MODEL), lambda i: (_mod_row(i * tm), 0, 4)),
            pl.BlockSpec((N_EXPERTS, D_MODEL), lambda i: (0, 0)),
            pl.BlockSpec((N_EXPERTS, 1), lambda i: (0, 0)),
            pl.BlockSpec((tm, tm), lambda i: (0, 0)),
        ],
        out_specs=(
            pl.BlockSpec((tm, D_MODEL), lambda i: (i, 0)),
            pl.BlockSpec((TOP_K, tm), lambda i: (0, i)),
            pl.BlockSpec((TOP_K, tm), lambda i: (0, i)),
            pl.BlockSpec((TOP_K, tm), lambda i: (0, i)),
            pl.BlockSpec((N_EXPERTS, 1), lambda i: (0, 0)),
        ),
        scratch_shapes=[pltpu.VMEM((N_EXPERTS, 1), F32)],
        compiler_params=_params(dimension_semantics=("arbitrary",)),
        name="norm2_router",
    )(x, g.reshape(1, D_MODEL), mods, mods, w_router_t, b_router.reshape(N_EXPERTS, 1), tri)


def _moe_kernel(ue_ref, urow_ref, un_ref, uf_ref, ua_ref, src_ref,
                h_hbm, wg_ref, wl_ref, bg_ref, bl_ref, wd_ref, bd_ref, out_hbm,
                xf, xb, acc, gsem, osem):
    u = pl.program_id(0)
    c = pl.program_id(1)
    nrows = un_ref[u]
    nfill = uf_ref[u]
    row0 = urow_ref[u]
    row0_next = urow_ref[jnp.minimum(u + 1, MOE_UNITS - 1)]

    def row_copy(r, tok):
        return pltpu.make_async_copy(h_hbm.at[pl.ds(tok, 1)], xf.at[pl.ds(r, 1)], gsem)

    def gather_sub(first_row, first_slot):
        for r in range(MOE_SUB):
            row_copy(first_slot + r, src_ref[first_row + r]).start()

    def out_copy(sub):
        rows = pl.ds(sub * MOE_SUB, MOE_SUB)
        dst = pl.ds(pl.multiple_of(row0 + sub * MOE_SUB, MOE_SUB), MOE_SUB)
        return pltpu.make_async_copy(acc.at[rows], out_hbm.at[dst], osem)

    @pl.when(c == 0)
    def _():
        @pl.when(u == 0)
        def _():
            for sub in range(MOE_NSUB):
                gather_sub(row0 + sub * MOE_SUB, sub * MOE_SUB)

        @pl.when(nrows > 0)
        def _():
            acc[...] = jnp.broadcast_to(bd_ref[...], acc.shape)

        @pl.when((u == 0) | (ua_ref[jnp.maximum(u - 1, 0)] > 0))
        def _():
            for r in range(MOE_UNIT):
                row_copy(r, 0).wait()

        for sub in range(MOE_NSUB):
            @pl.when(sub * MOE_SUB < nrows)
            def _():
                rows = pl.ds(sub * MOE_SUB, MOE_SUB)
                xb[rows, :] = xf[rows, :].astype(BF16)

    def expert_rows(sub, wg, wl, wd):
        rows = pl.ds(sub * MOE_SUB, MOE_SUB)
        xs = xb[rows, :]
        glu = jnp.dot(xs, wg, preferred_element_type=F32) + bg_ref[...]
        lin = jnp.dot(xs, wl, preferred_element_type=F32) + bl_ref[...]
        glu = jnp.minimum(glu, SWIGLU_LIMIT)
        lin = jnp.clip(lin, -SWIGLU_LIMIT, SWIGLU_LIMIT)
        hdn = glu * jax.nn.sigmoid(SWIGLU_ALPHA * glu) * (lin + 1.0)
        acc[rows, :] = acc[rows, :] + jnp.dot(hdn.astype(BF16), wd, preferred_element_type=F32)

    nsub = nrows // MOE_SUB
    for n in range(1, MOE_NSUB + 1):
        @pl.when(nsub == n)
        def _():
            wg = wg_ref[...].astype(BF16)
            wl = wl_ref[...].astype(BF16)
            wd = wd_ref[...].astype(BF16)
            gather_sub(row0_next + c * MOE_SUB, c * MOE_SUB)
            for sub in range(n):
                expert_rows(sub, wg, wl, wd)

    @pl.when(c == MOE_NCHUNK - 1)
    def _():
        @pl.when(nfill > 0)
        def _():
            acc[...] = jnp.zeros_like(acc)

        nout = nrows + nfill
        for sub in range(MOE_NSUB):
            @pl.when(sub * MOE_SUB < nout)
            def _():
                out_copy(sub).start()
        for sub in range(MOE_NSUB):
            @pl.when(sub * MOE_SUB < nout)
            def _():
                out_copy(sub).wait()


def _moe(h2, plan, w_gu, b_gu, w_down, b_down, layer):
    fc = MOE_FCHUNK

    def chunk(c, act, u):
        return jnp.where(act[u] > 0, c, MOE_NCHUNK - 1)

    def wg_map(u, c, ue, ur, un, uf, ua, sr):
        return (layer, ue[u], 0, chunk(c, ua, u))

    def wl_map(u, c, ue, ur, un, uf, ua, sr):
        return (layer, ue[u], 0, MOE_NCHUNK + chunk(c, ua, u))

    def wd_map(u, c, ue, ur, un, uf, ua, sr):
        return (layer, ue[u], chunk(c, ua, u), 0)

    def bd_map(u, c, ue, ur, un, uf, ua, sr):
        return (layer, ue[u], 0, 0)

    grid_spec = pltpu.PrefetchScalarGridSpec(
        num_scalar_prefetch=6,
        grid=(MOE_UNITS, MOE_NCHUNK),
        in_specs=[
            pl.BlockSpec(memory_space=pl.ANY),
            pl.BlockSpec((None, None, D_MODEL, fc), wg_map),
            pl.BlockSpec((None, None, D_MODEL, fc), wl_map),
            pl.BlockSpec((None, None, 1, fc), wg_map),
            pl.BlockSpec((None, None, 1, fc), wl_map),
            pl.BlockSpec((None, None, fc, D_MODEL), wd_map),
            pl.BlockSpec((None, None, 1, D_MODEL), bd_map),
        ],
        out_specs=pl.BlockSpec(memory_space=pl.ANY),
        scratch_shapes=[
            pltpu.VMEM((MOE_UNIT, D_MODEL), F32),
            pltpu.VMEM((MOE_UNIT, D_MODEL), BF16),
            pltpu.VMEM((MOE_UNIT, D_MODEL), F32),
            pltpu.SemaphoreType.DMA,
            pltpu.SemaphoreType.DMA,
        ],
    )
    return pl.pallas_call(
        _moe_kernel,
        out_shape=jax.ShapeDtypeStruct((MOE_ROWS, D_MODEL), F32),
        grid_spec=grid_spec,
        compiler_params=_params(dimension_semantics=("arbitrary", "arbitrary")),
        name="moe_experts",
    )(*plan, h2, w_gu, w_gu,
      b_gu.reshape(DEPTH, N_EXPERTS, 1, 2 * D_FF), b_gu.reshape(DEPTH, N_EXPERTS, 1, 2 * D_FF),
      w_down, b_down.reshape(DEPTH, N_EXPERTS, 1, D_MODEL))


def _combine_kernel(dest_ref, rows_hbm, gate_ref, x_ref, g2_ref, *rest, split):
    o_refs, (buf, sem) = rest[:-2], rest[-2:]
    tm = COMBINE_TM
    base = pl.program_id(0) * tm

    def row_copy(r, kk, d):
        return pltpu.make_async_copy(rows_hbm.at[pl.ds(d, 1)], buf.at[kk, pl.ds(r, 1)], sem)

    for r in range(tm):
        for kk in range(TOP_K):
            row_copy(r, kk, dest_ref[(base + r) * TOP_K + kk]).start()
    for r in range(tm):
        for kk in range(TOP_K):
            row_copy(r, kk, 0).wait()
    gates = gate_ref[...]
    y = gates[:, 0:1] * buf[0]
    for kk in range(1, TOP_K):
        y = y + gates[:, kk:kk + 1] * buf[kk]
    val = x_ref[...] + g2_ref[...] * y
    if not split:
        o_refs[0][...] = val
    else:
        @pl.when(base < T_CTX)
        def _():
            o_refs[0][...] = val

        @pl.when(base >= T_CTX)
        def _():
            o_refs[1][...] = val


def _combine(dest_flat, rows, gates_t, x, mods, split):
    tm = COMBINE_TM
    n_ctx = T_CTX // tm
    if split:
        out_shape = (jax.ShapeDtypeStruct((T_CTX, D_MODEL), F32),
                     jax.ShapeDtypeStruct((T_LAT, D_MODEL), F32))
        out_specs = (pl.BlockSpec((tm, D_MODEL), lambda i, d: (jnp.minimum(i, n_ctx - 1), 0)),
                     pl.BlockSpec((tm, D_MODEL), lambda i, d: (jnp.maximum(i - n_ctx, 0), 0)))
    else:
        out_shape = jax.ShapeDtypeStruct((T_ALL, D_MODEL), F32)
        out_specs = pl.BlockSpec((tm, D_MODEL), lambda i, d: (i, 0))
    grid_spec = pltpu.PrefetchScalarGridSpec(
        num_scalar_prefetch=1,
        grid=(T_ALL // tm,),
        in_specs=[
            pl.BlockSpec(memory_space=pl.ANY),
            pl.BlockSpec((tm, TOP_K), lambda i, d: (i, 0)),
            pl.BlockSpec((tm, D_MODEL), lambda i, d: (i, 0)),
            pl.BlockSpec((None, 1, D_MODEL), lambda i, d: (_mod_row(i * tm), 0, 5)),
        ],
        out_specs=out_specs,
        scratch_shapes=[pltpu.VMEM((TOP_K, tm, D_MODEL), F32), pltpu.SemaphoreType.DMA],
    )
    return pl.pallas_call(
        functools.partial(_combine_kernel, split=split),
        out_shape=out_shape,
        grid_spec=grid_spec,
        compiler_params=_params(dimension_semantics=("arbitrary",)),
        name="moe_combine",
    )(dest_flat, rows, gates_t, x, mods)


def _src_kernel(dest_ref, src_ref):
    step = 8

    def clear(i, carry):
        for j in range(step):
            src_ref[i * step + j] = 0
        return carry

    def put(i, carry):
        for j in range(step):
            src_ref[dest_ref[i * step + j]] = i * (step // TOP_K) + j // TOP_K
        return carry

    lax.fori_loop(0, MOE_SRC_LEN // step, clear, 0)
    lax.fori_loop(0, N_ASSIGN // step, put, 0)


def _moe_src(dest_flat):
    return pl.pallas_call(
        _src_kernel,
        out_shape=jax.ShapeDtypeStruct((MOE_SRC_LEN,), jnp.int32),
        in_specs=[pl.BlockSpec(memory_space=pltpu.SMEM)],
        out_specs=pl.BlockSpec(memory_space=pltpu.SMEM),
        name="moe_src_rows",
    )(dest_flat)


def _moe_plan(e4, r4, counts):
    cnt = counts[:, 0].astype(jnp.int32)
    pad = (cnt + MOE_SUB - 1) // MOE_SUB * MOE_SUB
    pad_end = jnp.cumsum(pad)
    pad_start = pad_end - pad
    hot = e4[:, :, None] == jnp.arange(N_EXPERTS, dtype=jnp.int32)
    dest = jnp.sum(jnp.where(hot, pad_start, 0), axis=-1) + r4
    n_units_e = (pad + MOE_UNIT - 1) // MOE_UNIT
    unit_end = jnp.cumsum(n_units_e)
    unit_start = unit_end - n_units_e
    n_units = unit_end[-1]
    uid = jnp.minimum(jnp.arange(MOE_UNITS, dtype=jnp.int32), n_units - 1)
    unit_e = jnp.minimum(jnp.sum((uid[:, None] >= unit_end[None, :]).astype(jnp.int32), axis=1),
                         N_EXPERTS - 1)
    mine = unit_e[:, None] == jnp.arange(N_EXPERTS, dtype=jnp.int32)
    pick = lambda tab: jnp.sum(jnp.where(mine, tab, 0), axis=1)
    local = uid - pick(unit_start)
    unit_row = pick(pad_start) + local * MOE_UNIT
    unit_act = (jnp.arange(MOE_UNITS, dtype=jnp.int32) < n_units).astype(jnp.int32)
    unit_n = jnp.clip(pick(pad) - local * MOE_UNIT, 0, MOE_UNIT) * unit_act
    fill_row = pad_end[-1] + (jnp.arange(MOE_UNITS, dtype=jnp.int32) - n_units) * MOE_UNIT
    unit_fill = jnp.clip(MOE_ROWS - fill_row, 0, MOE_UNIT) * (1 - unit_act)
    unit_row = jnp.where(unit_act > 0, unit_row, jnp.minimum(fill_row, MOE_ROWS - MOE_SUB))
    dest_flat = dest.T.reshape(-1)
    i32 = lambda a: a.astype(jnp.int32)
    return dest_flat, (unit_e, i32(unit_row), i32(unit_n), i32(unit_fill), unit_act,
                       _moe_src(dest_flat))


def _dft_tables(n):
    idx = np.outer(np.arange(n), np.arange(n)) % n
    ang = 2.0 * np.pi * idx / n
    s = 1.0 / np.sqrt(n)
    return np.cos(ang) * s, np.sin(ang) * s


def _rope_tables():
    t = np.arange(LAT_SEQ)
    nf = HEAD_DIM // 4
    inv = ROPE_BASE ** (-np.arange(nf, dtype=np.float64) / nf)
    ang = np.concatenate([(t // GRID_W)[:, None] * inv, (t % GRID_W)[:, None] * inv], axis=-1)
    lane = np.arange(LANES)
    cos = np.cos(ang)[:, lane % (HEAD_DIM // 2)]
    sin = np.sin(ang)[:, lane % (HEAD_DIM // 2)]
    sign = np.where((lane % HEAD_DIM) < HEAD_DIM // 2, -1.0, 1.0)
    return cos.astype(np.float32), (sin * sign).astype(np.float32)


def _na_bias(rpb):
    c = np.arange(GRID_W)
    c0 = np.clip(c - NA_WIN_W // 2, 0, GRID_W - NA_WIN_W)
    c_ok = (c[None, :] >= c0[:, None]) & (c[None, :] < c0[:, None] + NA_WIN_W)
    dc = c[None, :] - c[:, None] + NA_WIN_W - 1
    pick = (np.arange(2 * NA_WIN_W - 1)[:, None, None] == dc[None]) & c_ok[None]
    cols = jnp.einsum('hrd,dqk->hrqk', rpb.astype(F32), jnp.asarray(pick, F32),
                      precision=lax.Precision.HIGHEST)
    cols = jnp.where(jnp.asarray(c_ok)[None, None], cols, NEG_INF)
    edge = jnp.full((NA_HEADS, 1, GRID_W, GRID_W), NEG_INF, F32)
    return jnp.concatenate([jnp.concatenate([edge, cols], axis=1),
                            jnp.concatenate([cols, edge], axis=1)], axis=-1)


def _dup_heads(cache):
    b, p = cache.shape[:2]
    return jnp.broadcast_to(cache[:, :, :, None, :], (b, p, SW_KV_HEADS, 2, HEAD_DIM)).reshape(
        b, p, SW_KV_HEADS * LANES)


def kernel(x_prompt, x_sample, cache_na_k, cache_na_v, cache_sw_k, cache_sw_v, c, c_ctx,
           w_ada, b_ada, norm1_g, norm2_g, w_in, qn_na, kn_na, qn_sw, kn_sw, rpb_na, sink_sw,
           w_br_na, w_br_ft, w_br_sw, w_o, w_router, b_router, w_gu, b_gu, w_down, b_down):
    nq = NA_HEADS * HEAD_DIM
    x = (x_prompt.reshape(T_CTX, D_MODEL), x_sample.reshape(T_LAT, D_MODEL))
    cvecs = jnp.zeros((MOD_ROWS, D_MODEL), F32).at[:LAT_BATCH].set(c).at[CTX_MOD_ROW].set(c_ctx)
    mods_all = _ada(cvecs, w_ada, b_ada)

    cc, sc = _dft_tables(FT_GROUP_DIM)
    cl_ctx, sl_ctx = _dft_tables(CTX_SEQ)
    cl_lat, sl_lat = _dft_tables(LAT_SEQ)
    as_bf16 = lambda a: jnp.asarray(a, BF16)
    cos_t, sin_t = _rope_tables()
    bd = as_bf16(np.kron(np.eye(LANES // HEAD_DIM), np.ones((HEAD_DIM, HEAD_DIM))))
    tri = as_bf16(np.triu(np.ones((ROUTER_TM, ROUTER_TM)), 1))
    tile2 = lambda g: jnp.tile(g.reshape(1, HEAD_DIM), (1, LANES // HEAD_DIM))

    na_k, na_v, sw_k, sw_v = [], [], [], []
    for l in range(DEPTH):
        mods = mods_all[l].reshape(MOD_ROWS, 1, N_MOD)
        h = _prenorm(x, norm1_g[l], mods)
        z = _in_proj(h, w_in, l)
        qa, ka, qc, kc_n, kc_dup, vc_dup = _prep(z, tile2(qn_na[l]), tile2(kn_na[l]), tile2(qn_sw[l]),
                                                 tile2(kn_sw[l]), bd, jnp.asarray(cos_t),
                                                 jnp.asarray(sin_t))
        na_k.append(ka[:T_CTX].reshape(CTX_BATCH, CTX_SEQ, NA_HEADS, HEAD_DIM))
        na_v.append(z[:T_CTX, Z_VA:Z_VA + nq].reshape(CTX_BATCH, CTX_SEQ, NA_HEADS, HEAD_DIM))
        sw_k.append(kc_n[:T_CTX].reshape(CTX_BATCH, CTX_SEQ, SW_KV_HEADS, HEAD_DIM))
        sw_v.append(z[:T_CTX, Z_VC:Z_VC + SW_KV_HEADS * HEAD_DIM].reshape(
            CTX_BATCH, CTX_SEQ, SW_KV_HEADS, HEAD_DIM))

        ctx = dict(n_batch=CTX_BATCH, seq=CTX_SEQ, row0=0, tq=CTX_SEQ, pp=4)
        lat = dict(n_batch=LAT_BATCH, seq=LAT_SEQ, row0=T_CTX, tq=256, pp=2)
        o_init = jnp.zeros((T_ALL, nq), BF16)
        o_a = _attention(qa, ka, z, v_col0=Z_VA // (LANES * ctx['pp']), out_prev=o_init,
                         name="attn_na_ctx", **ctx)
        o_a = _attention(qa, ka, z, v_col0=Z_VA // (LANES * lat['pp']),
                         cache_k=cache_na_k[:, l].reshape(LAT_BATCH, PAST_LEN, nq),
                         cache_v=cache_na_v[:, l].reshape(LAT_BATCH, PAST_LEN, nq),
                         bias=_na_bias(rpb_na[l]), kwin=NA_KEY_ROWS * GRID_W, out_prev=o_a,
                         name="attn_na_lat", **lat)
        o_c = _attention(qc, kc_dup, vc_dup, kv_div=2, sink=sink_sw[l], out_prev=o_init,
                         name="attn_sw_ctx", **ctx)
        o_c = _attention(qc, kc_dup, vc_dup, kv_div=2, sink=sink_sw[l],
                         cache_k=_dup_heads(cache_sw_k[:, l]), cache_v=_dup_heads(cache_sw_v[:, l]),
                         band=True, kwin=lat['tq'] + 2 * SW_WINDOW, out_prev=o_c, name="attn_sw_lat", **lat)
        as_f32 = lambda a: jnp.asarray(a, F32)
        o_b = _fourier(z, as_f32(cc), as_f32(-sc), as_f32(cl_ctx), as_f32(sl_ctx),
                       n_batch=CTX_BATCH, seq=CTX_SEQ, row0=0, out_prev=o_init, name="fourier_ctx")
        o_b = _fourier(z, as_f32(cc), as_f32(-sc), as_f32(cl_lat), as_f32(sl_lat),
                       n_batch=LAT_BATCH, seq=LAT_SEQ, row0=T_CTX, out_prev=o_b, name="fourier_lat")
        m = _merge1(o_a, o_b, o_c, w_br_na, w_br_ft, w_br_sw, z, l)
        x = _merge2(m, w_o, x, mods, l)

        h2, e4, g4, r4, counts = _router(x, norm2_g[l], mods, w_router[l].T, b_router[l], tri)
        dest_flat, plan = _moe_plan(e4, r4, counts)
        rows = _moe(h2, plan, w_gu, b_gu, w_down, b_down, l)
        x = _combine(dest_flat, rows, g4.T, x, mods, split=(l == DEPTH - 1))

    y_prompt = x[0].reshape(CTX_BATCH, CTX_SEQ, D_MODEL)
    y_sample = x[1].reshape(LAT_BATCH, LAT_SEQ, D_MODEL)
    return (y_prompt, y_sample, jnp.stack(na_k, axis=1), jnp.stack(na_v, axis=1),
            jnp.stack(sw_k, axis=1), jnp.stack(sw_v, axis=1))
```

```python
import functools

import numpy as np
import jax
import jax.numpy as jnp
from jax import lax
from jax.experimental import pallas as pl
from jax.experimental.pallas import tpu as pltpu

F32 = jnp.float32
BF16 = jnp.bfloat16

D_MODEL = 2048
DEPTH = 2
CTX_BATCH = 32
CTX_SEQ = 256
LAT_BATCH = 4
LAT_SEQ = 1024
PAST_LEN = 256
T_CTX = CTX_BATCH * CTX_SEQ
T_LAT = LAT_BATCH * LAT_SEQ
T_ALL = T_CTX + T_LAT
GRID_W = 64
HEAD_DIM = 64
NA_HEADS = 16
NA_WIN_H = 8
NA_WIN_W = 16
FT_GROUPS = 4
FT_GROUP_DIM = 256
SW_HEADS = 16
SW_KV_HEADS = 4
SW_WINDOW = 128
ROPE_BASE = 10000.0
N_EXPERTS = 32
TOP_K = 4
D_FF = 2048
SWIGLU_LIMIT = 7.0
SWIGLU_ALPHA = 1.702
EPS = 1e-6
NEG_INF = -1e30
NA_KEY_ROWS = 12

Z_QA, Z_KA, Z_VA, Z_UB, Z_QC, Z_KC, Z_VC, Z_GATE = 0, 1024, 2048, 3072, 4096, 5120, 5376, 5632
IN_WIDTH = Z_GATE + 3 * D_MODEL
N_MOD = 6 * D_MODEL
CTX_MOD_ROW = LAT_BATCH
MOD_ROWS = 8

LANES = 128
VMEM_LIMIT = 56 * 1024 * 1024

MOE_SUB = 256
MOE_UNIT = 1024
MOE_NSUB = MOE_UNIT // MOE_SUB
MOE_FCHUNK = 512
MOE_NCHUNK = D_FF // MOE_FCHUNK
N_ASSIGN = T_ALL * TOP_K
MOE_ROWS = -(-(N_ASSIGN + N_EXPERTS * (MOE_SUB - 1)) // MOE_UNIT) * MOE_UNIT
MOE_UNITS = MOE_ROWS // MOE_UNIT + N_EXPERTS
MOE_SRC_LEN = MOE_ROWS + MOE_UNIT
ROUTER_TM = 256
COMBINE_TM = 256


def _params(**kw):
    return pltpu.CompilerParams(vmem_limit_bytes=VMEM_LIMIT, **kw)


def _mod_row(row0):
    return jnp.where(row0 < T_CTX, CTX_MOD_ROW, (row0 - T_CTX) // LAT_SEQ)


def _split(a):
    hi = a.astype(BF16)
    lo = (a - hi.astype(F32)).astype(BF16)
    return hi, lo


def _dot3(a, b, dims):
    ah, al = _split(a)
    bh, bl = _split(b)
    d = functools.partial(lax.dot_general, dimension_numbers=dims, preferred_element_type=F32)
    return d(ah, bh) + d(ah, bl) + d(al, bh)


_NN = (((1,), (0,)), ((), ()))
_NT = (((1,), (1,)), ((), ()))


def _ada_kernel(c_ref, w_ref, b_ref, o_ref):
    c = c_ref[...]
    s = c * jax.nn.sigmoid(c)
    o_ref[...] = _dot3(s, w_ref[...], _NN) + b_ref[...]


def _ada(cvecs, w_ada, b_ada):
    tn = 1024
    return pl.pallas_call(
        _ada_kernel,
        out_shape=jax.ShapeDtypeStruct((DEPTH, MOD_ROWS, N_MOD), F32),
        grid=(DEPTH, N_MOD // tn),
        in_specs=[
            pl.BlockSpec((MOD_ROWS, D_MODEL), lambda l, j: (0, 0)),
            pl.BlockSpec((None, D_MODEL, tn), lambda l, j: (l, 0, j)),
            pl.BlockSpec((None, 1, tn), lambda l, j: (l, 0, j)),
        ],
        out_specs=pl.BlockSpec((None, MOD_ROWS, tn), lambda l, j: (l, 0, j)),
        compiler_params=_params(),
        name="ada_table",
    )(cvecs, w_ada, b_ada.reshape(DEPTH, 1, N_MOD))


def _rms_mod(x, g, shift, scale):
    y = x * lax.rsqrt(jnp.mean(x * x, axis=-1, keepdims=True) + EPS)
    return (y * g) * (1.0 + scale) + shift


def _token_specs(x, tm, tn):
    col = lambda j: j[0] if j else 0
    if not isinstance(x, tuple):
        return [pl.BlockSpec((tm, tn), lambda i, *j: (i, col(j)))], [x]
    n_ctx = T_CTX // tm
    return ([pl.BlockSpec((tm, tn), lambda i, *j: (jnp.minimum(i, n_ctx - 1),
                                                   jnp.where(i < n_ctx, col(j), 0))),
             pl.BlockSpec((tm, tn), lambda i, *j: (jnp.maximum(i - n_ctx, 0),
                                                   jnp.where(i >= n_ctx, col(j), 0)))], list(x))


def _token_tile(refs, tm):
    if len(refs) == 1:
        return refs[0][...]
    return jnp.where(pl.program_id(0) * tm >= T_CTX, refs[1][...], refs[0][...])


def _prenorm_kernel(*refs, tm):
    g_ref, sh_ref, sc_ref, h_ref = refs[-4:]
    x = _token_tile(refs[:-4], tm)
    h_ref[...] = _rms_mod(x, g_ref[...], sh_ref[...], sc_ref[...]).astype(h_ref.dtype)


def _prenorm(x, g, mods):
    tm = 512
    x_specs, x_args = _token_specs(x, tm, D_MODEL)
    return pl.pallas_call(
        functools.partial(_prenorm_kernel, tm=tm),
        out_shape=jax.ShapeDtypeStruct((T_ALL, D_MODEL), BF16),
        grid=(T_ALL // tm,),
        in_specs=x_specs + [
            pl.BlockSpec((1, D_MODEL), lambda i: (0, 0)),
            pl.BlockSpec((None, 1, D_MODEL), lambda i: (_mod_row(i * tm), 0, 0)),
            pl.BlockSpec((None, 1, D_MODEL), lambda i: (_mod_row(i * tm), 0, 1)),
        ],
        out_specs=pl.BlockSpec((tm, D_MODEL), lambda i: (i, 0)),
        compiler_params=_params(),
        name="prenorm",
    )(*x_args, g.reshape(1, D_MODEL), mods, mods)


def _mm_kernel(a_ref, w_ref, o_ref):
    o_ref[...] = jnp.dot(a_ref[...], w_ref[...].astype(BF16),
                         preferred_element_type=F32).astype(o_ref.dtype)


def _in_proj(h, w_in, layer):
    tm, tn = 2048, 512
    return pl.pallas_call(
        _mm_kernel,
        out_shape=jax.ShapeDtypeStruct((T_ALL, IN_WIDTH), F32),
        grid=(T_ALL // tm, IN_WIDTH // tn),
        in_specs=[
            pl.BlockSpec((tm, D_MODEL), lambda i, j: (i, 0)),
            pl.BlockSpec((None, D_MODEL, tn), lambda i, j: (layer, 0, j)),
        ],
        out_specs=pl.BlockSpec((tm, tn), lambda i, j: (i, j)),
        compiler_params=_params(),
        name="in_proj",
    )(h, w_in)


def _head_norm(x, gain, bd):
    hi, lo = _split(x * x)
    ss = jnp.dot(hi, bd, preferred_element_type=F32) + jnp.dot(lo, bd, preferred_element_type=F32)
    return x * lax.rsqrt(ss * (1.0 / HEAD_DIM) + EPS) * gain


def _prep_kernel(qa_ref, ka_ref, va_ref, qc_ref, kc_ref, vc_ref, gqa_ref, gka_ref, gqc_ref, gkc_ref,
                 bd_ref, cos_ref, sin_ref, nk_in, nv_in, sk_in, sv_in,
                 qa_o, ka_o, qc_o, kcd_o, vcd_o, nk_o, nv_o, sk_o, sv_o, *, tm):
    del nk_in, nv_in, sk_in, sv_in
    is_lat = pl.program_id(0) * tm >= T_CTX
    is_ctx = jnp.logical_not(is_lat)
    lane = lax.broadcasted_iota(jnp.int32, (1, LANES), 1)
    first_half = (lane % HEAD_DIM) < (HEAD_DIM // 2)
    low_head = lane < HEAD_DIM
    bd = bd_ref[...]
    cos = jnp.where(is_lat, cos_ref[...], 1.0)
    sin = jnp.where(is_lat, sin_ref[...], 0.0)

    def rope(y):
        partner = jnp.where(first_half, pltpu.roll(y, LANES - HEAD_DIM // 2, 1),
                            pltpu.roll(y, HEAD_DIM // 2, 1))
        return y * cos + partner * sin

    def keep(out_ref, sl, val):
        @pl.when(is_ctx)
        def _():
            out_ref[:, sl] = val

    for s in range(NA_HEADS * HEAD_DIM // LANES):
        sl = slice(s * LANES, (s + 1) * LANES)
        qa_o[:, sl] = _head_norm(qa_ref[:, sl], gqa_ref[...], bd).astype(qa_o.dtype)
        kan = _head_norm(ka_ref[:, sl], gka_ref[...], bd)
        ka_o[:, sl] = kan.astype(ka_o.dtype)
        keep(nk_o, sl, kan)
        qc_o[:, sl] = rope(_head_norm(qc_ref[:, sl], gqc_ref[...], bd)).astype(qc_o.dtype)
    keep(nv_o, slice(None), va_ref[...])
    for s in range(SW_KV_HEADS * HEAD_DIM // LANES):
        sl = slice(s * LANES, (s + 1) * LANES)
        kn = _head_norm(kc_ref[:, sl], gkc_ref[...], bd)
        keep(sk_o, sl, kn)
        kr = rope(kn)
        v = vc_ref[:, sl]
        keep(sv_o, sl, v)
        kr_sw = pltpu.roll(kr, HEAD_DIM, 1)
        v_sw = pltpu.roll(v, HEAD_DIM, 1)
        dt = kcd_o.dtype
        kcd_o[:, (2 * s) * LANES:(2 * s + 1) * LANES] = jnp.where(low_head, kr, kr_sw).astype(dt)
        kcd_o[:, (2 * s + 1) * LANES:(2 * s + 2) * LANES] = jnp.where(low_head, kr_sw, kr).astype(dt)
        vcd_o[:, (2 * s) * LANES:(2 * s + 1) * LANES] = jnp.where(low_head, v, v_sw).astype(dt)
        vcd_o[:, (2 * s + 1) * LANES:(2 * s + 2) * LANES] = jnp.where(low_head, v_sw, v).astype(dt)


def _prep(z, gqa, gka, gqc, gkc, bd, cos_t, sin_t, caches, layer):
    tm = 256
    nq = NA_HEADS * HEAD_DIM
    nkv = SW_KV_HEADS * HEAD_DIM
    lat_tiles = LAT_SEQ // tm
    n_ctx = T_CTX // tm
    assert tm == CTX_SEQ

    def pos_map(i):
        return (jnp.maximum(i - n_ctx, 0) % lat_tiles, 0)

    def cache_spec(width):
        return pl.BlockSpec((None, None, tm, width), lambda i: (jnp.minimum(i, n_ctx - 1), layer, 0, 0))

    gain = pl.BlockSpec((1, LANES), lambda i: (0, 0))
    alias = pl.BlockSpec(memory_space=pl.ANY)
    n_in = 13
    return pl.pallas_call(
        functools.partial(_prep_kernel, tm=tm),
        out_shape=(
            jax.ShapeDtypeStruct((T_ALL, nq), BF16),
            jax.ShapeDtypeStruct((T_ALL, nq), BF16),
            jax.ShapeDtypeStruct((T_ALL, nq), BF16),
            jax.ShapeDtypeStruct((T_ALL, 2 * nkv), BF16),
            jax.ShapeDtypeStruct((T_ALL, 2 * nkv), BF16),
        ) + tuple(jax.ShapeDtypeStruct(c.shape, c.dtype) for c in caches),
        grid=(T_ALL // tm,),
        in_specs=[
            pl.BlockSpec((tm, nq), lambda i: (i, Z_QA // nq)),
            pl.BlockSpec((tm, nq), lambda i: (i, Z_KA // nq)),
            pl.BlockSpec((tm, nq), lambda i: (jnp.minimum(i, n_ctx - 1), Z_VA // nq)),
            pl.BlockSpec((tm, nq), lambda i: (i, Z_QC // nq)),
            pl.BlockSpec((tm, nkv), lambda i: (i, Z_KC // nkv)),
            pl.BlockSpec((tm, nkv), lambda i: (i, Z_VC // nkv)),
            gain, gain, gain, gain,
            pl.BlockSpec((LANES, LANES), lambda i: (0, 0)),
            pl.BlockSpec((tm, LANES), pos_map),
            pl.BlockSpec((tm, LANES), pos_map),
            alias, alias, alias, alias,
        ],
        out_specs=(
            pl.BlockSpec((tm, nq), lambda i: (i, 0)),
            pl.BlockSpec((tm, nq), lambda i: (i, 0)),
            pl.BlockSpec((tm, nq), lambda i: (i, 0)),
            pl.BlockSpec((tm, 2 * nkv), lambda i: (i, 0)),
            pl.BlockSpec((tm, 2 * nkv), lambda i: (i, 0)),
            cache_spec(nq), cache_spec(nq), cache_spec(nkv), cache_spec(nkv),
        ),
        input_output_aliases={n_in + k: 5 + k for k in range(4)},
        compiler_params=_params(),
        name="qk_prep",
    )(z, z, z, z, z, z, gqa, gka, gqc, gkc, bd, cos_t, sin_t, *caches)


def _attn_kernel(*refs, tq, tk, kwin, pp, kv_div, has_cache, has_bias, has_sink, band, aliased):
    it = iter(refs)
    q_ref, k_ref, v_ref = next(it), next(it), next(it)
    kc_ref = vc_ref = bias_ref = sink_ref = None
    if has_cache:
        kc_ref, vc_ref = next(it), next(it)
    if has_bias:
        bias_ref = next(it)
    if has_sink:
        sink_ref = next(it)
    if aliased:
        next(it)
    o_ref = next(it)
    bias_scr = next(it) if has_bias else None

    group = pl.program_id(0)
    qi = pl.program_id(1)
    lane = lax.broadcasted_iota(jnp.int32, (1, LANES), 1)
    krows = slice(None)
    if band:
        kstart = pl.multiple_of(jnp.clip(qi * tq - SW_WINDOW, 0, tk - kwin), LANES)
        krows = pl.ds(kstart, kwin)
        qpos = qi * tq + lax.broadcasted_iota(jnp.int32, (tq, 1), 0)
        kpos = kstart + lax.broadcasted_iota(jnp.int32, (1, kwin), 1)
        in_band = jnp.abs(qpos - kpos) <= SW_WINDOW
    if has_bias:
        first_row = jnp.clip(qi * (tq // GRID_W) - NA_WIN_H // 2, 0, (tk - kwin) // GRID_W)
        krows = pl.ds(pl.multiple_of(first_row * GRID_W, LANES), kwin)

        @pl.when(pl.program_id(2) == 0)
        def _():
            _na_bias_tile(bias_ref, bias_scr, qi, tq, pp, lane, first_row, kwin)

    for j in range(pp):
        qs = slice(j * LANES, (j + 1) * LANES)
        ks = slice((j // kv_div) * LANES, (j // kv_div + 1) * LANES)
        q = q_ref[:, qs] * (HEAD_DIM ** -0.5)
        k = k_ref[krows, ks].astype(BF16)
        v = v_ref[krows, ks]
        if has_cache:
            kc = kc_ref[:, ks].astype(BF16)
            vc = vc_ref[:, ks]
        acc = jnp.zeros((tq, LANES), F32)
        for half in range(2):
            hm = (lane < HEAD_DIM) if half == 0 else (lane >= HEAD_DIM)
            qm = jnp.where(hm, q, 0.0).astype(BF16)
            s = lax.dot_general(qm, k, _NT, preferred_element_type=F32)
            if has_bias:
                s = s + bias_scr[j, half]
            if band:
                s = jnp.where(in_band, s, NEG_INF)
            m = jnp.max(s, axis=1, keepdims=True)
            if has_cache:
                sc = lax.dot_general(qm, kc, _NT, preferred_element_type=F32)
                m = jnp.maximum(m, jnp.max(sc, axis=1, keepdims=True))
            if has_sink:
                sk = sink_ref[2 * (group * pp + j) + half]
                m = jnp.maximum(m, sk)
            p = jnp.exp(s - m)
            denom = jnp.sum(p, axis=1, keepdims=True)
            o = jnp.dot(p.astype(BF16), jnp.where(hm, v, 0.0).astype(BF16),
                        preferred_element_type=F32)
            if has_cache:
                pc = jnp.exp(sc - m)
                denom = denom + jnp.sum(pc, axis=1, keepdims=True)
                o = o + jnp.dot(pc.astype(BF16), jnp.where(hm, vc, 0.0).astype(BF16),
                                preferred_element_type=F32)
            if has_sink:
                denom = denom + jnp.exp(sk - m)
            acc = acc + o / denom
        o_ref[:, qs] = acc.astype(o_ref.dtype)


def _na_bias_tile(tab_ref, out_ref, qi, tq, pp, lane, first_row, kwin):
    rows = LAT_SEQ // GRID_W
    wh = min(NA_WIN_H, rows)
    for a in range(tq // GRID_W):
        rq = qi * (tq // GRID_W) + a
        r0 = jnp.clip(rq - wh // 2, 0, rows - wh)
        for kp in range(kwin // LANES):
            rk = first_row + 2 * kp
            ok_lo = ((rk >= r0) & (rk < r0 + wh)).astype(jnp.int32)
            ok_hi = ((rk + 1 >= r0) & (rk + 1 < r0 + wh)).astype(jnp.int32)
            ok = jnp.where(lane < GRID_W, ok_lo, ok_hi) > 0
            idx = jnp.clip(rk - rq + NA_WIN_H, 0, 2 * NA_WIN_H - 1)
            for hh in range(2 * pp):
                tile = jnp.where(ok, tab_ref[hh, idx], NEG_INF)
                out_ref[hh // 2, hh % 2, a * GRID_W:(a + 1) * GRID_W, kp * LANES:(kp + 1) * LANES] = tile


def _attention(q, k, v, *, n_batch, seq, row0, tq, pp, q_col0=0, k_col0=0, v_col0=0, kv_div=1,
               cache_k=None, cache_v=None, bias=None, sink=None, band=False, kwin=None,
               out_prev=None, name):
    kwin = seq if kwin is None else kwin
    n_groups = NA_HEADS * HEAD_DIM // (LANES * pp)
    nqt = seq // tq
    qw = LANES * pp
    kw = qw // kv_div
    has_cache, has_bias, has_sink = cache_k is not None, bias is not None, sink is not None
    aliased = out_prev is not None

    in_specs = [
        pl.BlockSpec((tq, qw), lambda g, qi, b: (row0 // tq + b * nqt + qi, q_col0 + g)),
        pl.BlockSpec((seq, kw), lambda g, qi, b: (row0 // seq + b, k_col0 + g)),
        pl.BlockSpec((seq, kw), lambda g, qi, b: (row0 // seq + b, v_col0 + g)),
    ]
    args = [q, k, v]
    if has_cache:
        cspec = pl.BlockSpec((None, PAST_LEN, kw), lambda g, qi, b: (b, 0, g))
        in_specs += [cspec, cspec]
        args += [cache_k, cache_v]
    if has_bias:
        in_specs.append(pl.BlockSpec((2 * pp, 2 * NA_WIN_H, GRID_W, LANES), lambda g, qi, b: (g, 0, 0, 0)))
        args.append(bias)
    if has_sink:
        in_specs.append(pl.BlockSpec(memory_space=pltpu.SMEM))
        args.append(sink)
    if aliased:
        in_specs.append(pl.BlockSpec(memory_space=pl.ANY))
        args.append(out_prev)
    return pl.pallas_call(
        functools.partial(_attn_kernel, tq=tq, tk=seq, kwin=kwin, pp=pp, kv_div=kv_div, has_cache=has_cache,
                          has_bias=has_bias, has_sink=has_sink, band=band, aliased=aliased),
        out_shape=jax.ShapeDtypeStruct((T_ALL, NA_HEADS * HEAD_DIM), BF16),
        grid=(n_groups, nqt, n_batch),
        in_specs=in_specs,
        out_specs=pl.BlockSpec((tq, qw), lambda g, qi, b: (row0 // tq + b * nqt + qi, g)),
        input_output_aliases={len(args) - 1: 0} if aliased else {},
        scratch_shapes=[pltpu.VMEM((pp, 2, tq, kwin), F32)] if has_bias else [],
        compiler_params=_params(),
        name=name,
    )(*args)


def _fourier_kernel(*refs, aliased):
    u_ref, cc_ref, sc_ref, cl_ref, sl_ref = refs[:5]
    o_ref = refs[5 + int(aliased)]
    cc_b, sc_b, cl_b, sl_b = refs[6 + int(aliased):]

    @pl.when((pl.program_id(0) == 0) & (pl.program_id(1) == 0))
    def _():
        cc_b[...] = cc_ref[...].astype(BF16)
        sc_b[...] = sc_ref[...].astype(BF16)
        cl_b[...] = cl_ref[...].astype(BF16)
        sl_b[...] = sl_ref[...].astype(BF16)

    u = u_ref[...].astype(BF16)
    a = jnp.dot(u, cc_b[...], preferred_element_type=F32).astype(BF16)
    b = jnp.dot(u, sc_b[...], preferred_element_type=F32).astype(BF16)
    o = (jnp.dot(cl_b[...], a, preferred_element_type=F32)
         + jnp.dot(sl_b[...], b, preferred_element_type=F32))
    o_ref[...] = o.astype(o_ref.dtype)


def _fourier(z, cc, sc, cl, sl, *, n_batch, seq, row0, out_prev=None, name):
    aliased = out_prev is not None
    gd = FT_GROUP_DIM
    in_specs = [
        pl.BlockSpec((seq, gd), lambda b, g: (row0 // seq + b, Z_UB // gd + g)),
        pl.BlockSpec((gd, gd), lambda b, g: (0, 0)),
        pl.BlockSpec((gd, gd), lambda b, g: (0, 0)),
        pl.BlockSpec((seq, seq), lambda b, g: (0, 0)),
        pl.BlockSpec((seq, seq), lambda b, g: (0, 0)),
    ]
    args = [z, cc, sc, cl, sl]
    if aliased:
        in_specs.append(pl.BlockSpec(memory_space=pl.ANY))
        args.append(out_prev)
    return pl.pallas_call(
        functools.partial(_fourier_kernel, aliased=aliased),
        out_shape=jax.ShapeDtypeStruct((T_ALL, FT_GROUPS * gd), BF16),
        grid=(n_batch, FT_GROUPS),
        in_specs=in_specs,
        out_specs=pl.BlockSpec((seq, gd), lambda b, g: (row0 // seq + b, g)),
        input_output_aliases={len(args) - 1: 0} if aliased else {},
        scratch_shapes=[pltpu.VMEM((gd, gd), BF16), pltpu.VMEM((gd, gd), BF16),
                        pltpu.VMEM((seq, seq), BF16), pltpu.VMEM((seq, seq), BF16)],
        compiler_params=_params(dimension_semantics=("arbitrary", "arbitrary")),
        name=name,
    )(*args)


def _merge1_kernel(oa, ob, oc, wa, wb, wc, ga, gb, gc, m_ref):
    def branch(o, w, g):
        return jax.nn.sigmoid(g[...]) * jnp.dot(o[...], w[...].astype(BF16),
                                                preferred_element_type=F32)
    m_ref[...] = (branch(oa, wa, ga) + branch(ob, wb, gb) + branch(oc, wc, gc)).astype(m_ref.dtype)


def _merge1(o_a, o_b, o_c, w_a, w_b, w_c, z, layer):
    tm, tn = 1024, 512
    kdim = NA_HEADS * HEAD_DIM
    o_spec = pl.BlockSpec((tm, kdim), lambda i, j: (i, 0))
    w_spec = pl.BlockSpec((None, kdim, tn), lambda i, j: (layer, 0, j))

    def gate_spec(br):
        return pl.BlockSpec((tm, tn), lambda i, j: (i, (Z_GATE + br * D_MODEL) // tn + j))

    return pl.pallas_call(
        _merge1_kernel,
        out_shape=jax.ShapeDtypeStruct((T_ALL, D_MODEL), BF16),
        grid=(T_ALL // tm, D_MODEL // tn),
        in_specs=[o_spec, o_spec, o_spec, w_spec, w_spec, w_spec,
                  gate_spec(0), gate_spec(1), gate_spec(2)],
        out_specs=pl.BlockSpec((tm, tn), lambda i, j: (i, j)),
        compiler_params=_params(),
        name="merge_branches",
    )(o_a, o_b, o_c, w_a, w_b, w_c, z, z, z)


def _merge2_kernel(m_ref, w_ref, g_ref, *refs, tm):
    o_ref = refs[-1]
    x = _token_tile(refs[:-1], tm)
    o_ref[...] = x + g_ref[...] * jnp.dot(m_ref[...], w_ref[...].astype(BF16),
                                          preferred_element_type=F32)


def _merge2(m, w_o, x, mods, layer):
    tm, tn = 1024, 512
    x_specs, x_args = _token_specs(x, tm, tn)
    return pl.pallas_call(
        functools.partial(_merge2_kernel, tm=tm),
        out_shape=jax.ShapeDtypeStruct((T_ALL, D_MODEL), F32),
        grid=(T_ALL // tm, D_MODEL // tn),
        in_specs=[
            pl.BlockSpec((tm, D_MODEL), lambda i, j: (i, 0)),
            pl.BlockSpec((None, D_MODEL, tn), lambda i, j: (layer, 0, j)),
            pl.BlockSpec((None, 1, tn), lambda i, j: (_mod_row(i * tm), 0, 2 * D_MODEL // tn + j)),
        ] + x_specs,
        out_specs=pl.BlockSpec((tm, tn), lambda i, j: (i, j)),
        compiler_params=_params(),
        name="out_proj_residual",
    )(m, w_o, mods, *x_args)


def _router_kernel(x_ref, g_ref, sh_ref, sc_ref, wr_ref, br_ref, tri_ref,
                   h_ref, e_ref, gate_ref, rank_ref, cnt_ref, carry):
    @pl.when(pl.program_id(0) == 0)
    def _():
        carry[...] = jnp.zeros_like(carry)

    h = _rms_mod(x_ref[...], g_ref[...], sh_ref[...], sc_ref[...])
    h_ref[...] = h
    tm = h.shape[0]
    logits = _dot3(wr_ref[...], h, _NT) + br_ref[...]
    eidx = lax.broadcasted_iota(jnp.int32, (N_EXPERTS, tm), 0).astype(F32)
    work = logits
    vals, hots = [], []
    for kk in range(TOP_K):
        m = jnp.max(work, axis=0, keepdims=True)
        idx = jnp.min(jnp.where(work == m, eidx, float(N_EXPERTS)), axis=0, keepdims=True)
        hot = eidx == idx
        work = jnp.where(hot, -jnp.inf, work)
        vals.append(m)
        hots.append(hot)
        e_ref[kk:kk + 1, :] = idx.astype(jnp.int32)
    ex = [jnp.exp(vv - vals[0]) for vv in vals]
    den = ex[0] + ex[1] + ex[2] + ex[3]
    for kk in range(TOP_K):
        gate_ref[kk:kk + 1, :] = ex[kk] / den
    chosen = jnp.zeros((N_EXPERTS, tm), F32)
    for hot in hots:
        chosen = chosen + jnp.where(hot, 1.0, 0.0)
    before = jnp.dot(chosen.astype(BF16), tri_ref[...], preferred_element_type=F32) + carry[...]
    for kk in range(TOP_K):
        rk = jnp.sum(jnp.where(hots[kk], before, 0.0), axis=0, keepdims=True)
        rank_ref[kk:kk + 1, :] = rk.astype(jnp.int32)
    carry[...] = carry[...] + jnp.sum(chosen, axis=1, keepdims=True)
    cnt_ref[...] = carry[...]


def _router(x, g, mods, w_router_t, b_router, tri):
    tm = ROUTER_TM
    return pl.pallas_call(
        _router_kernel,
        out_shape=(
            jax.ShapeDtypeStruct((T_ALL, D_MODEL), F32),
            jax.ShapeDtypeStruct((TOP_K, T_ALL), jnp.int32),
            jax.ShapeDtypeStruct((TOP_K, T_ALL), F32),
            jax.ShapeDtypeStruct((TOP_K, T_ALL), jnp.int32),
            jax.ShapeDtypeStruct((N_EXPERTS, 1), F32),
        ),
        grid=(T_ALL // tm,),
        in_specs=[
            pl.BlockSpec((tm, D_MODEL), lambda i: (i, 0)),
            pl.BlockSpec((1, D_MODEL), lambda i: (0, 0)),
            pl.BlockSpec((None, 1, D_MODEL), lambda i: (_mod_row(i * tm), 0, 3)),
            pl.BlockSpec((None, 1, D_MODEL), lambda i: (_mod_row(i * tm), 0, 4)),
            pl.BlockSpec((N_EXPERTS, D_MODEL), lambda i: (0, 0)),
            pl.BlockSpec((N_EXPERTS, 1), lambda i: (0, 0)),
            pl.BlockSpec((tm, tm), lambda i: (0, 0)),
        ],
        out_specs=(
            pl.BlockSpec((tm, D_MODEL), lambda i: (i, 0)),
            pl.BlockSpec((TOP_K, tm), lambda i: (0, i)),
            pl.BlockSpec((TOP_K, tm), lambda i: (0, i)),
            pl.BlockSpec((TOP_K, tm), lambda i: (0, i)),
            pl.BlockSpec((N_EXPERTS, 1), lambda i: (0, 0)),
        ),
        scratch_shapes=[pltpu.VMEM((N_EXPERTS, 1), F32)],
        compiler_params=_params(dimension_semantics=("arbitrary",)),
        name="norm2_router",
    )(x, g.reshape(1, D_MODEL), mods, mods, w_router_t, b_router.reshape(N_EXPERTS, 1), tri)


def _moe_kernel(ue_ref, urow_ref, un_ref, uf_ref, ua_ref, src_ref,
                h_hbm, wg_ref, wl_ref, bg_ref, bl_ref, wd_ref, bd_ref, out_hbm,
                xf, xb, acc, gsem, osem):
    u = pl.program_id(0)
    c = pl.program_id(1)
    nrows = un_ref[u]
    nfill = uf_ref[u]
    row0 = urow_ref[u]
    row0_next = urow_ref[jnp.minimum(u + 1, MOE_UNITS - 1)]

    def row_copy(r, tok):
        return pltpu.make_async_copy(h_hbm.at[pl.ds(tok, 1)], xf.at[pl.ds(r, 1)], gsem)

    def gather_sub(first_row, first_slot):
        for r in range(MOE_SUB):
            row_copy(first_slot + r, src_ref[first_row + r]).start()

    def out_copy(sub):
        rows = pl.ds(sub * MOE_SUB, MOE_SUB)
        dst = pl.ds(pl.multiple_of(row0 + sub * MOE_SUB, MOE_SUB), MOE_SUB)
        return pltpu.make_async_copy(acc.at[rows], out_hbm.at[dst], osem)

    @pl.when(c == 0)
    def _():
        @pl.when(u == 0)
        def _():
            for sub in range(MOE_NSUB):
                gather_sub(row0 + sub * MOE_SUB, sub * MOE_SUB)

        @pl.when(nrows > 0)
        def _():
            acc[...] = jnp.broadcast_to(bd_ref[...], acc.shape)

        @pl.when((u == 0) | (ua_ref[jnp.maximum(u - 1, 0)] > 0))
        def _():
            for r in range(MOE_UNIT):
                row_copy(r, 0).wait()

        for sub in range(MOE_NSUB):
            @pl.when(sub * MOE_SUB < nrows)
            def _():
                rows = pl.ds(sub * MOE_SUB, MOE_SUB)
                xb[rows, :] = xf[rows, :].astype(BF16)

    def expert_rows(sub, wg, wl, wd):
        rows = pl.ds(sub * MOE_SUB, MOE_SUB)
        xs = xb[rows, :]
        glu = jnp.dot(xs, wg, preferred_element_type=F32) + bg_ref[...]
        lin = jnp.dot(xs, wl, preferred_element_type=F32) + bl_ref[...]
        glu = jnp.minimum(glu, SWIGLU_LIMIT)
        lin = jnp.clip(lin, -SWIGLU_LIMIT, SWIGLU_LIMIT)
        hdn = glu * jax.nn.sigmoid(SWIGLU_ALPHA * glu) * (lin + 1.0)
        acc[rows, :] = acc[rows, :] + jnp.dot(hdn.astype(BF16), wd, preferred_element_type=F32)

    nsub = nrows // MOE_SUB
    for n in range(1, MOE_NSUB + 1):
        @pl.when(nsub == n)
        def _():
            wg = wg_ref[...].astype(BF16)
            wl = wl_ref[...].astype(BF16)
            wd = wd_ref[...].astype(BF16)
            gather_sub(row0_next + c * MOE_SUB, c * MOE_SUB)
            for sub in range(n):
                expert_rows(sub, wg, wl, wd)

    @pl.when(c == MOE_NCHUNK - 1)
    def _():
        @pl.when(nfill > 0)
        def _():
            acc[...] = jnp.zeros_like(acc)

        nout = nrows + nfill
        for sub in range(MOE_NSUB):
            @pl.when(sub * MOE_SUB < nout)
            def _():
                out_copy(sub).start()
        for sub in range(MOE_NSUB):
            @pl.when(sub * MOE_SUB < nout)
            def _():
                out_copy(sub).wait()


def _moe(h2, plan, w_gu, b_gu, w_down, b_down, layer):
    fc = MOE_FCHUNK

    def chunk(c, act, u):
        return jnp.where(act[u] > 0, c, MOE_NCHUNK - 1)

    def wg_map(u, c, ue, ur, un, uf, ua, sr):
        return (layer, ue[u], 0, chunk(c, ua, u))

    def wl_map(u, c, ue, ur, un, uf, ua, sr):
        return (layer, ue[u], 0, MOE_NCHUNK + chunk(c, ua, u))

    def wd_map(u, c, ue, ur, un, uf, ua, sr):
        return (layer, ue[u], chunk(c, ua, u), 0)

    def bd_map(u, c, ue, ur, un, uf, ua, sr):
        return (layer, ue[u], 0, 0)

    grid_spec = pltpu.PrefetchScalarGridSpec(
        num_scalar_prefetch=6,
        grid=(MOE_UNITS, MOE_NCHUNK),
        in_specs=[
            pl.BlockSpec(memory_space=pl.ANY),
            pl.BlockSpec((None, None, D_MODEL, fc), wg_map),
            pl.BlockSpec((None, None, D_MODEL, fc), wl_map),
            pl.BlockSpec((None, None, 1, fc), wg_map),
            pl.BlockSpec((None, None, 1, fc), wl_map),
            pl.BlockSpec((None, None, fc, D_MODEL), wd_map),
            pl.BlockSpec((None, None, 1, D_MODEL), bd_map),
        ],
        out_specs=pl.BlockSpec(memory_space=pl.ANY),
        scratch_shapes=[
            pltpu.VMEM((MOE_UNIT, D_MODEL), F32),
            pltpu.VMEM((MOE_UNIT, D_MODEL), BF16),
            pltpu.VMEM((MOE_UNIT, D_MODEL), F32),
            pltpu.SemaphoreType.DMA,
            pltpu.SemaphoreType.DMA,
        ],
    )
    return pl.pallas_call(
        _moe_kernel,
        out_shape=jax.ShapeDtypeStruct((MOE_ROWS, D_MODEL), F32),
        grid_spec=grid_spec,
        compiler_params=_params(dimension_semantics=("arbitrary", "arbitrary")),
        name="moe_experts",
    )(*plan, h2, w_gu, w_gu,
      b_gu.reshape(DEPTH, N_EXPERTS, 1, 2 * D_FF), b_gu.reshape(DEPTH, N_EXPERTS, 1, 2 * D_FF),
      w_down, b_down.reshape(DEPTH, N_EXPERTS, 1, D_MODEL))


def _combine_kernel(dest_ref, rows_hbm, gate_ref, x_ref, g2_ref, *rest, split):
    o_refs, (buf, sem) = rest[:-2], rest[-2:]
    tm = COMBINE_TM
    base = pl.program_id(0) * tm

    def row_copy(r, kk, d):
        return pltpu.make_async_copy(rows_hbm.at[pl.ds(d, 1)], buf.at[kk, pl.ds(r, 1)], sem)

    for r in range(tm):
        for kk in range(TOP_K):
            row_copy(r, kk, dest_ref[(base + r) * TOP_K + kk]).start()
    for r in range(tm):
        for kk in range(TOP_K):
            row_copy(r, kk, 0).wait()
    gates = gate_ref[...]
    y = gates[:, 0:1] * buf[0]
    for kk in range(1, TOP_K):
        y = y + gates[:, kk:kk + 1] * buf[kk]
    val = x_ref[...] + g2_ref[...] * y
    if not split:
        o_refs[0][...] = val
    else:
        @pl.when(base < T_CTX)
        def _():
            o_refs[0][...] = val

        @pl.when(base >= T_CTX)
        def _():
            o_refs[1][...] = val


def _combine(dest_flat, rows, gates_t, x, mods, split):
    tm = COMBINE_TM
    n_ctx = T_CTX // tm
    if split:
        out_shape = (jax.ShapeDtypeStruct((T_CTX, D_MODEL), F32),
                     jax.ShapeDtypeStruct((T_LAT, D_MODEL), F32))
        out_specs = (pl.BlockSpec((tm, D_MODEL), lambda i, d: (jnp.minimum(i, n_ctx - 1), 0)),
                     pl.BlockSpec((tm, D_MODEL), lambda i, d: (jnp.maximum(i - n_ctx, 0), 0)))
    else:
        out_shape = jax.ShapeDtypeStruct((T_ALL, D_MODEL), F32)
        out_specs = pl.BlockSpec((tm, D_MODEL), lambda i, d: (i, 0))
    grid_spec = pltpu.PrefetchScalarGridSpec(
        num_scalar_prefetch=1,
        grid=(T_ALL // tm,),
        in_specs=[
            pl.BlockSpec(memory_space=pl.ANY),
            pl.BlockSpec((tm, TOP_K), lambda i, d: (i, 0)),
            pl.BlockSpec((tm, D_MODEL), lambda i, d: (i, 0)),
            pl.BlockSpec((None, 1, D_MODEL), lambda i, d: (_mod_row(i * tm), 0, 5)),
        ],
        out_specs=out_specs,
        scratch_shapes=[pltpu.VMEM((TOP_K, tm, D_MODEL), F32), pltpu.SemaphoreType.DMA],
    )
    return pl.pallas_call(
        functools.partial(_combine_kernel, split=split),
        out_shape=out_shape,
        grid_spec=grid_spec,
        compiler_params=_params(dimension_semantics=("arbitrary",)),
        name="moe_combine",
    )(dest_flat, rows, gates_t, x, mods)


def _src_kernel(dest_ref, src_ref):
    step = 8

    def clear(i, carry):
        for j in range(step):
            src_ref[i * step + j] = 0
        return carry

    def put(i, carry):
        for j in range(step):
            src_ref[dest_ref[i * step + j]] = i * (step // TOP_K) + j // TOP_K
        return carry

    lax.fori_loop(0, MOE_SRC_LEN // step, clear, 0)
    lax.fori_loop(0, N_ASSIGN // step, put, 0)


def _moe_src(dest_flat):
    return pl.pallas_call(
        _src_kernel,
        out_shape=jax.ShapeDtypeStruct((MOE_SRC_LEN,), jnp.int32),
        in_specs=[pl.BlockSpec(memory_space=pltpu.SMEM)],
        out_specs=pl.BlockSpec(memory_space=pltpu.SMEM),
        name="moe_src_rows",
    )(dest_flat)


def _moe_plan(e4, r4, counts):
    cnt = counts[:, 0].astype(jnp.int32)
    pad = (cnt + MOE_SUB - 1) // MOE_SUB * MOE_SUB
    pad_end = jnp.cumsum(pad)
    pad_start = pad_end - pad
    hot = e4[:, :, None] == jnp.arange(N_EXPERTS, dtype=jnp.int32)
    dest = jnp.sum(jnp.where(hot, pad_start, 0), axis=-1) + r4
    n_units_e = (pad + MOE_UNIT - 1) // MOE_UNIT
    unit_end = jnp.cumsum(n_units_e)
    unit_start = unit_end - n_units_e
    n_units = unit_end[-1]
    uid = jnp.minimum(jnp.arange(MOE_UNITS, dtype=jnp.int32), n_units - 1)
    unit_e = jnp.minimum(jnp.sum((uid[:, None] >= unit_end[None, :]).astype(jnp.int32), axis=1),
                         N_EXPERTS - 1)
    mine = unit_e[:, None] == jnp.arange(N_EXPERTS, dtype=jnp.int32)
    pick = lambda tab: jnp.sum(jnp.where(mine, tab, 0), axis=1)
    local = uid - pick(unit_start)
    unit_row = pick(pad_start) + local * MOE_UNIT
    unit_act = (jnp.arange(MOE_UNITS, dtype=jnp.int32) < n_units).astype(jnp.int32)
    unit_n = jnp.clip(pick(pad) - local * MOE_UNIT, 0, MOE_UNIT) * unit_act
    fill_row = pad_end[-1] + (jnp.arange(MOE_UNITS, dtype=jnp.int32) - n_units) * MOE_UNIT
    unit_fill = jnp.clip(MOE_ROWS - fill_row, 0, MOE_UNIT) * (1 - unit_act)
    unit_row = jnp.where(unit_act > 0, unit_row, jnp.minimum(fill_row, MOE_ROWS - MOE_SUB))
    dest_flat = dest.T.reshape(-1)
    i32 = lambda a: a.astype(jnp.int32)
    return dest_flat, (unit_e, i32(unit_row), i32(unit_n), i32(unit_fill), unit_act,
                       _moe_src(dest_flat))


def _dft_tables(n):
    idx = np.outer(np.arange(n), np.arange(n)) % n
    ang = 2.0 * np.pi * idx / n
    s = 1.0 / np.sqrt(n)
    return np.cos(ang) * s, np.sin(ang) * s


def _rope_tables():
    t = np.arange(LAT_SEQ)
    nf = HEAD_DIM // 4
    inv = ROPE_BASE ** (-np.arange(nf, dtype=np.float64) / nf)
    ang = np.concatenate([(t // GRID_W)[:, None] * inv, (t % GRID_W)[:, None] * inv], axis=-1)
    lane = np.arange(LANES)
    cos = np.cos(ang)[:, lane % (HEAD_DIM // 2)]
    sin = np.sin(ang)[:, lane % (HEAD_DIM // 2)]
    sign = np.where((lane % HEAD_DIM) < HEAD_DIM // 2, -1.0, 1.0)
    return cos.astype(np.float32), (sin * sign).astype(np.float32)


def _na_bias(rpb):
    c = np.arange(GRID_W)
    c0 = np.clip(c - NA_WIN_W // 2, 0, GRID_W - NA_WIN_W)
    c_ok = (c[None, :] >= c0[:, None]) & (c[None, :] < c0[:, None] + NA_WIN_W)
    dc = c[None, :] - c[:, None] + NA_WIN_W - 1
    pick = (np.arange(2 * NA_WIN_W - 1)[:, None, None] == dc[None]) & c_ok[None]
    cols = jnp.einsum('hrd,dqk->hrqk', rpb.astype(F32), jnp.asarray(pick, F32),
                      precision=lax.Precision.HIGHEST)
    cols = jnp.where(jnp.asarray(c_ok)[None, None], cols, NEG_INF)
    edge = jnp.full((NA_HEADS, 1, GRID_W, GRID_W), NEG_INF, F32)
    return jnp.concatenate([jnp.concatenate([edge, cols], axis=1),
                            jnp.concatenate([cols, edge], axis=1)], axis=-1)


def _dup_heads(cache):
    b, p = cache.shape[:2]
    return jnp.broadcast_to(cache[:, :, :, None, :], (b, p, SW_KV_HEADS, 2, HEAD_DIM)).reshape(
        b, p, SW_KV_HEADS * LANES)


def kernel(x_prompt, x_sample, cache_na_k, cache_na_v, cache_sw_k, cache_sw_v, c, c_ctx,
           w_ada, b_ada, norm1_g, norm2_g, w_in, qn_na, kn_na, qn_sw, kn_sw, rpb_na, sink_sw,
           w_br_na, w_br_ft, w_br_sw, w_o, w_router, b_router, w_gu, b_gu, w_down, b_down):
    nq = NA_HEADS * HEAD_DIM
    x = (x_prompt.reshape(T_CTX, D_MODEL), x_sample.reshape(T_LAT, D_MODEL))
    cvecs = jnp.zeros((MOD_ROWS, D_MODEL), F32).at[:LAT_BATCH].set(c).at[CTX_MOD_ROW].set(c_ctx)
    mods_all = _ada(cvecs, w_ada, b_ada)

    cc, sc = _dft_tables(FT_GROUP_DIM)
    cl_ctx, sl_ctx = _dft_tables(CTX_SEQ)
    cl_lat, sl_lat = _dft_tables(LAT_SEQ)
    as_bf16 = lambda a: jnp.asarray(a, BF16)
    cos_t, sin_t = _rope_tables()
    bd = as_bf16(np.kron(np.eye(LANES // HEAD_DIM), np.ones((HEAD_DIM, HEAD_DIM))))
    tri = as_bf16(np.triu(np.ones((ROUTER_TM, ROUTER_TM)), 1))
    tile2 = lambda g: jnp.tile(g.reshape(1, HEAD_DIM), (1, LANES // HEAD_DIM))

    caches = [jnp.zeros((CTX_BATCH, DEPTH, CTX_SEQ, w), F32)
              for w in (nq, nq, SW_KV_HEADS * HEAD_DIM, SW_KV_HEADS * HEAD_DIM)]
    for l in range(DEPTH):
        mods = mods_all[l].reshape(MOD_ROWS, 1, N_MOD)
        h = _prenorm(x, norm1_g[l], mods)
        z = _in_proj(h, w_in, l)
        qa, ka, qc, kc_dup, vc_dup, *caches = _prep(
            z, tile2(qn_na[l]), tile2(kn_na[l]), tile2(qn_sw[l]), tile2(kn_sw[l]), bd,
            jnp.asarray(cos_t), jnp.asarray(sin_t), caches, l)

        ctx = dict(n_batch=CTX_BATCH, seq=CTX_SEQ, row0=0, tq=CTX_SEQ, pp=8)
        lat = dict(n_batch=LAT_BATCH, seq=LAT_SEQ, row0=T_CTX, tq=256, pp=2)
        o_init = jnp.zeros((T_ALL, nq), BF16)
        o_a = _attention(qa, ka, z, v_col0=Z_VA // (LANES * ctx['pp']), out_prev=o_init,
                         name="attn_na_ctx", **ctx)
        o_a = _attention(qa, ka, z, v_col0=Z_VA // (LANES * lat['pp']),
                         cache_k=cache_na_k[:, l].reshape(LAT_BATCH, PAST_LEN, nq),
                         cache_v=cache_na_v[:, l].reshape(LAT_BATCH, PAST_LEN, nq),
                         bias=_na_bias(rpb_na[l]), kwin=NA_KEY_ROWS * GRID_W, out_prev=o_a,
                         name="attn_na_lat", **lat)
        o_c = _attention(qc, kc_dup, vc_dup, kv_div=2, sink=sink_sw[l], out_prev=o_init,
                         name="attn_sw_ctx", **ctx)
        o_c = _attention(qc, kc_dup, vc_dup, kv_div=2, sink=sink_sw[l],
                         cache_k=_dup_heads(cache_sw_k[:, l]), cache_v=_dup_heads(cache_sw_v[:, l]),
                         band=True, kwin=lat['tq'] + 2 * SW_WINDOW, out_prev=o_c, name="attn_sw_lat", **lat)
        as_f32 = lambda a: jnp.asarray(a, F32)
        o_b = _fourier(z, as_f32(cc), as_f32(-sc), as_f32(cl_ctx), as_f32(sl_ctx),
                       n_batch=CTX_BATCH, seq=CTX_SEQ, row0=0, out_prev=o_init, name="fourier_ctx")
        o_b = _fourier(z, as_f32(cc), as_f32(-sc), as_f32(cl_lat), as_f32(sl_lat),
                       n_batch=LAT_BATCH, seq=LAT_SEQ, row0=T_CTX, out_prev=o_b, name="fourier_lat")
        m = _merge1(o_a, o_b, o_c, w_br_na, w_br_ft, w_br_sw, z, l)
        x = _merge2(m, w_o, x, mods, l)

        h2, e4, g4, r4, counts = _router(x, norm2_g[l], mods, w_router[l].T, b_router[l], tri)
        dest_flat, plan = _moe_plan(e4, r4, counts)
        rows = _moe(h2, plan, w_gu, b_gu, w_down, b_down, l)
        x = _combine(dest_flat, rows, g4.T, x, mods, split=(l == DEPTH - 1))

    y_prompt = x[0].reshape(CTX_BATCH, CTX_SEQ, D_MODEL)
    y_sample = x[1].reshape(LAT_BATCH, LAT_SEQ, D_MODEL)
    heads = lambda c, n: c.reshape(CTX_BATCH, DEPTH, CTX_SEQ, n, HEAD_DIM)
    return (y_prompt, y_sample, heads(caches[0], NA_HEADS), heads(caches[1], NA_HEADS),
            heads(caches[2], SW_KV_HEADS), heads(caches[3], SW_KV_HEADS))
```

```python
import functools

import numpy as np
import jax
import jax.numpy as jnp
from jax import lax
from jax.experimental import pallas as pl
from jax.experimental.pallas import tpu as pltpu

F32 = jnp.float32
BF16 = jnp.bfloat16

D_MODEL = 2048
DEPTH = 2
CTX_BATCH = 32
CTX_SEQ = 256
LAT_BATCH = 4
LAT_SEQ = 1024
PAST_LEN = 256
T_CTX = CTX_BATCH * CTX_SEQ
T_LAT = LAT_BATCH * LAT_SEQ
T_ALL = T_CTX + T_LAT
GRID_W = 64
HEAD_DIM = 64
NA_HEADS = 16
NA_WIN_H = 8
NA_WIN_W = 16
FT_GROUPS = 4
FT_GROUP_DIM = 256
SW_HEADS = 16
SW_KV_HEADS = 4
SW_WINDOW = 128
ROPE_BASE = 10000.0
N_EXPERTS = 32
TOP_K = 4
D_FF = 2048
SWIGLU_LIMIT = 7.0
SWIGLU_ALPHA = 1.702
EPS = 1e-6
NEG_INF = -1e30
NA_KEY_ROWS = 12

Z_QA, Z_KA, Z_VA, Z_UB, Z_QC, Z_KC, Z_VC, Z_GATE = 0, 1024, 2048, 3072, 4096, 5120, 5376, 5632
IN_WIDTH = Z_GATE + 3 * D_MODEL
N_MOD = 6 * D_MODEL
CTX_MOD_ROW = LAT_BATCH
MOD_ROWS = 8

LANES = 128
VMEM_LIMIT = 56 * 1024 * 1024
MOE_VMEM_LIMIT = 62 * 1024 * 1024

MOE_SUB = 256
MOE_UNIT = 1024
MOE_NSUB = MOE_UNIT // MOE_SUB
MOE_FCHUNK = 512
MOE_NCHUNK = D_FF // MOE_FCHUNK
N_ASSIGN = T_ALL * TOP_K
MOE_ROWS = -(-(N_ASSIGN + N_EXPERTS * (MOE_SUB - 1)) // MOE_UNIT) * MOE_UNIT
MOE_UNITS = MOE_ROWS // MOE_UNIT + N_EXPERTS
MOE_SRC_LEN = MOE_ROWS + MOE_UNIT
ROUTER_TM = 256
COMBINE_TM = 256


def _params(**kw):
    return pltpu.CompilerParams(vmem_limit_bytes=VMEM_LIMIT, **kw)


def _mod_row(row0):
    return jnp.where(row0 < T_CTX, CTX_MOD_ROW, (row0 - T_CTX) // LAT_SEQ)


def _split(a):
    hi = a.astype(BF16)
    lo = (a - hi.astype(F32)).astype(BF16)
    return hi, lo


def _dot3(a, b, dims):
    ah, al = _split(a)
    bh, bl = _split(b)
    d = functools.partial(lax.dot_general, dimension_numbers=dims, preferred_element_type=F32)
    return d(ah, bh) + d(ah, bl) + d(al, bh)


_NN = (((1,), (0,)), ((), ()))
_NT = (((1,), (1,)), ((), ()))


def _ada_kernel(c_ref, w_ref, b_ref, o_ref):
    c = c_ref[...]
    s = c * jax.nn.sigmoid(c)
    o_ref[...] = _dot3(s, w_ref[...], _NN) + b_ref[...]


def _ada(cvecs, w_ada, b_ada):
    tn = 1024
    return pl.pallas_call(
        _ada_kernel,
        out_shape=jax.ShapeDtypeStruct((DEPTH, MOD_ROWS, N_MOD), F32),
        grid=(DEPTH, N_MOD // tn),
        in_specs=[
            pl.BlockSpec((MOD_ROWS, D_MODEL), lambda l, j: (0, 0)),
            pl.BlockSpec((None, D_MODEL, tn), lambda l, j: (l, 0, j)),
            pl.BlockSpec((None, 1, tn), lambda l, j: (l, 0, j)),
        ],
        out_specs=pl.BlockSpec((None, MOD_ROWS, tn), lambda l, j: (l, 0, j)),
        compiler_params=_params(),
        name="ada_table",
    )(cvecs, w_ada, b_ada.reshape(DEPTH, 1, N_MOD))


def _rms_mod(x, g, shift, scale):
    y = x * lax.rsqrt(jnp.mean(x * x, axis=-1, keepdims=True) + EPS)
    return (y * g) * (1.0 + scale) + shift


def _token_specs(x, tm, tn):
    col = lambda j: j[0] if j else 0
    if not isinstance(x, tuple):
        return [pl.BlockSpec((tm, tn), lambda i, *j: (i, col(j)))], [x]
    n_ctx = T_CTX // tm
    return ([pl.BlockSpec((tm, tn), lambda i, *j: (jnp.minimum(i, n_ctx - 1),
                                                   jnp.where(i < n_ctx, col(j), 0))),
             pl.BlockSpec((tm, tn), lambda i, *j: (jnp.maximum(i - n_ctx, 0),
                                                   jnp.where(i >= n_ctx, col(j), 0)))], list(x))


def _token_tile(refs, tm):
    if len(refs) == 1:
        return refs[0][...]
    return jnp.where(pl.program_id(0) * tm >= T_CTX, refs[1][...], refs[0][...])


def _prenorm_kernel(*refs, tm):
    g_ref, sh_ref, sc_ref, h_ref = refs[-4:]
    x = _token_tile(refs[:-4], tm)
    h_ref[...] = _rms_mod(x, g_ref[...], sh_ref[...], sc_ref[...]).astype(h_ref.dtype)


def _prenorm(x, g, mods):
    tm = 512
    x_specs, x_args = _token_specs(x, tm, D_MODEL)
    return pl.pallas_call(
        functools.partial(_prenorm_kernel, tm=tm),
        out_shape=jax.ShapeDtypeStruct((T_ALL, D_MODEL), BF16),
        grid=(T_ALL // tm,),
        in_specs=x_specs + [
            pl.BlockSpec((1, D_MODEL), lambda i: (0, 0)),
            pl.BlockSpec((None, 1, D_MODEL), lambda i: (_mod_row(i * tm), 0, 0)),
            pl.BlockSpec((None, 1, D_MODEL), lambda i: (_mod_row(i * tm), 0, 1)),
        ],
        out_specs=pl.BlockSpec((tm, D_MODEL), lambda i: (i, 0)),
        compiler_params=_params(),
        name="prenorm",
    )(*x_args, g.reshape(1, D_MODEL), mods, mods)


def _mm_kernel(a_ref, w_ref, o_ref):
    o_ref[...] = jnp.dot(a_ref[...], w_ref[...].astype(BF16),
                         preferred_element_type=F32).astype(o_ref.dtype)


def _in_proj(h, w_in, layer):
    tm, tn = 2048, 512
    return pl.pallas_call(
        _mm_kernel,
        out_shape=jax.ShapeDtypeStruct((T_ALL, IN_WIDTH), F32),
        grid=(T_ALL // tm, IN_WIDTH // tn),
        in_specs=[
            pl.BlockSpec((tm, D_MODEL), lambda i, j: (i, 0)),
            pl.BlockSpec((None, D_MODEL, tn), lambda i, j: (layer, 0, j)),
        ],
        out_specs=pl.BlockSpec((tm, tn), lambda i, j: (i, j)),
        compiler_params=_params(),
        name="in_proj",
    )(h, w_in)


def _head_norm(x, gain, bd):
    hi, lo = _split(x * x)
    ss = jnp.dot(hi, bd, preferred_element_type=F32) + jnp.dot(lo, bd, preferred_element_type=F32)
    return x * lax.rsqrt(ss * (1.0 / HEAD_DIM) + EPS) * gain


def _prep_kernel(qa_ref, ka_ref, va_ref, qc_ref, kc_ref, vc_ref, gqa_ref, gka_ref, gqc_ref, gkc_ref,
                 bd_ref, cos_ref, sin_ref, nk_in, nv_in, sk_in, sv_in,
                 qa_o, ka_o, qc_o, kcd_o, vcd_o, nk_o, nv_o, sk_o, sv_o, *, tm):
    del nk_in, nv_in, sk_in, sv_in
    is_lat = pl.program_id(0) * tm >= T_CTX
    is_ctx = jnp.logical_not(is_lat)
    lane = lax.broadcasted_iota(jnp.int32, (1, LANES), 1)
    first_half = (lane % HEAD_DIM) < (HEAD_DIM // 2)
    low_head = lane < HEAD_DIM
    bd = bd_ref[...]
    cos = jnp.where(is_lat, cos_ref[...], 1.0)
    sin = jnp.where(is_lat, sin_ref[...], 0.0)

    def rope(y):
        partner = jnp.where(first_half, pltpu.roll(y, LANES - HEAD_DIM // 2, 1),
                            pltpu.roll(y, HEAD_DIM // 2, 1))
        return y * cos + partner * sin

    def keep(out_ref, sl, val):
        @pl.when(is_ctx)
        def _():
            out_ref[:, sl] = val

    for s in range(NA_HEADS * HEAD_DIM // LANES):
        sl = slice(s * LANES, (s + 1) * LANES)
        qa_o[:, sl] = _head_norm(qa_ref[:, sl], gqa_ref[...], bd).astype(qa_o.dtype)
        kan = _head_norm(ka_ref[:, sl], gka_ref[...], bd)
        ka_o[:, sl] = kan.astype(ka_o.dtype)
        keep(nk_o, sl, kan)
        qc_o[:, sl] = rope(_head_norm(qc_ref[:, sl], gqc_ref[...], bd)).astype(qc_o.dtype)
    keep(nv_o, slice(None), va_ref[...])
    for s in range(SW_KV_HEADS * HEAD_DIM // LANES):
        sl = slice(s * LANES, (s + 1) * LANES)
        kn = _head_norm(kc_ref[:, sl], gkc_ref[...], bd)
        keep(sk_o, sl, kn)
        kr = rope(kn)
        v = vc_ref[:, sl]
        keep(sv_o, sl, v)
        kr_sw = pltpu.roll(kr, HEAD_DIM, 1)
        v_sw = pltpu.roll(v, HEAD_DIM, 1)
        dt = kcd_o.dtype
        kcd_o[:, (2 * s) * LANES:(2 * s + 1) * LANES] = jnp.where(low_head, kr, kr_sw).astype(dt)
        kcd_o[:, (2 * s + 1) * LANES:(2 * s + 2) * LANES] = jnp.where(low_head, kr_sw, kr).astype(dt)
        vcd_o[:, (2 * s) * LANES:(2 * s + 1) * LANES] = jnp.where(low_head, v, v_sw).astype(dt)
        vcd_o[:, (2 * s + 1) * LANES:(2 * s + 2) * LANES] = jnp.where(low_head, v_sw, v).astype(dt)


def _prep(z, gqa, gka, gqc, gkc, bd, cos_t, sin_t, caches, layer):
    tm = 256
    nq = NA_HEADS * HEAD_DIM
    nkv = SW_KV_HEADS * HEAD_DIM
    lat_tiles = LAT_SEQ // tm
    n_ctx = T_CTX // tm
    assert tm == CTX_SEQ

    def pos_map(i):
        return (jnp.maximum(i - n_ctx, 0) % lat_tiles, 0)

    def cache_spec(width):
        return pl.BlockSpec((None, None, tm, width), lambda i: (jnp.minimum(i, n_ctx - 1), layer, 0, 0))

    gain = pl.BlockSpec((1, LANES), lambda i: (0, 0))
    alias = pl.BlockSpec(memory_space=pl.ANY)
    n_in = 13
    return pl.pallas_call(
        functools.partial(_prep_kernel, tm=tm),
        out_shape=(
            jax.ShapeDtypeStruct((T_ALL, nq), BF16),
            jax.ShapeDtypeStruct((T_ALL, nq), BF16),
            jax.ShapeDtypeStruct((T_ALL, nq), BF16),
            jax.ShapeDtypeStruct((T_ALL, 2 * nkv), BF16),
            jax.ShapeDtypeStruct((T_ALL, 2 * nkv), BF16),
        ) + tuple(jax.ShapeDtypeStruct(c.shape, c.dtype) for c in caches),
        grid=(T_ALL // tm,),
        in_specs=[
            pl.BlockSpec((tm, nq), lambda i: (i, Z_QA // nq)),
            pl.BlockSpec((tm, nq), lambda i: (i, Z_KA // nq)),
            pl.BlockSpec((tm, nq), lambda i: (jnp.minimum(i, n_ctx - 1), Z_VA // nq)),
            pl.BlockSpec((tm, nq), lambda i: (i, Z_QC // nq)),
            pl.BlockSpec((tm, nkv), lambda i: (i, Z_KC // nkv)),
            pl.BlockSpec((tm, nkv), lambda i: (i, Z_VC // nkv)),
            gain, gain, gain, gain,
            pl.BlockSpec((LANES, LANES), lambda i: (0, 0)),
            pl.BlockSpec((tm, LANES), pos_map),
            pl.BlockSpec((tm, LANES), pos_map),
            alias, alias, alias, alias,
        ],
        out_specs=(
            pl.BlockSpec((tm, nq), lambda i: (i, 0)),
            pl.BlockSpec((tm, nq), lambda i: (i, 0)),
            pl.BlockSpec((tm, nq), lambda i: (i, 0)),
            pl.BlockSpec((tm, 2 * nkv), lambda i: (i, 0)),
            pl.BlockSpec((tm, 2 * nkv), lambda i: (i, 0)),
            cache_spec(nq), cache_spec(nq), cache_spec(nkv), cache_spec(nkv),
        ),
        input_output_aliases={n_in + k: 5 + k for k in range(4)},
        compiler_params=_params(),
        name="qk_prep",
    )(z, z, z, z, z, z, gqa, gka, gqc, gkc, bd, cos_t, sin_t, *caches)


def _attn_kernel(*refs, tq, tk, kwin, pp, kv_div, has_cache, has_bias, has_sink, band, aliased):
    it = iter(refs)
    q_ref, k_ref, v_ref = next(it), next(it), next(it)
    kc_ref = vc_ref = bias_ref = sink_ref = None
    if has_cache:
        kc_ref, vc_ref = next(it), next(it)
    if has_bias:
        bias_ref = next(it)
    if has_sink:
        sink_ref = next(it)
    if aliased:
        next(it)
    o_ref = next(it)
    bias_scr = next(it) if has_bias else None

    group = pl.program_id(0)
    qi = pl.program_id(1)
    lane = lax.broadcasted_iota(jnp.int32, (1, LANES), 1)
    krows = slice(None)
    if band:
        kstart = pl.multiple_of(jnp.clip(qi * tq - SW_WINDOW, 0, tk - kwin), LANES)
        krows = pl.ds(kstart, kwin)
        qpos = qi * tq + lax.broadcasted_iota(jnp.int32, (tq, 1), 0)
        kpos = kstart + lax.broadcasted_iota(jnp.int32, (1, kwin), 1)
        in_band = jnp.abs(qpos - kpos) <= SW_WINDOW
    if has_bias:
        first_row = jnp.clip(qi * (tq // GRID_W) - NA_WIN_H // 2, 0, (tk - kwin) // GRID_W)
        krows = pl.ds(pl.multiple_of(first_row * GRID_W, LANES), kwin)

        @pl.when(pl.program_id(2) == 0)
        def _():
            _na_bias_tile(bias_ref, bias_scr, qi, tq, pp, lane, first_row, kwin)

    for j in range(pp):
        qs = slice(j * LANES, (j + 1) * LANES)
        ks = slice((j // kv_div) * LANES, (j // kv_div + 1) * LANES)
        q = q_ref[:, qs] * (HEAD_DIM ** -0.5)
        k = k_ref[krows, ks].astype(BF16)
        v = v_ref[krows, ks]
        if has_cache:
            kc = kc_ref[:, ks].astype(BF16)
            vc = vc_ref[:, ks]
        acc = jnp.zeros((tq, LANES), F32)
        for half in range(2):
            hm = (lane < HEAD_DIM) if half == 0 else (lane >= HEAD_DIM)
            qm = jnp.where(hm, q, 0.0).astype(BF16)
            s = lax.dot_general(qm, k, _NT, preferred_element_type=F32)
            if has_bias:
                s = s + bias_scr[j, half]
            if band:
                s = jnp.where(in_band, s, NEG_INF)
            m = jnp.max(s, axis=1, keepdims=True)
            if has_cache:
                sc = lax.dot_general(qm, kc, _NT, preferred_element_type=F32)
                m = jnp.maximum(m, jnp.max(sc, axis=1, keepdims=True))
            if has_sink:
                sk = sink_ref[2 * (group * pp + j) + half]
                m = jnp.maximum(m, sk)
            p = jnp.exp(s - m)
            denom = jnp.sum(p, axis=1, keepdims=True)
            o = jnp.dot(p.astype(BF16), jnp.where(hm, v, 0.0).astype(BF16),
                        preferred_element_type=F32)
            if has_cache:
                pc = jnp.exp(sc - m)
                denom = denom + jnp.sum(pc, axis=1, keepdims=True)
                o = o + jnp.dot(pc.astype(BF16), jnp.where(hm, vc, 0.0).astype(BF16),
                                preferred_element_type=F32)
            if has_sink:
                denom = denom + jnp.exp(sk - m)
            acc = acc + o / denom
        o_ref[:, qs] = acc.astype(o_ref.dtype)


def _na_bias_tile(tab_ref, out_ref, qi, tq, pp, lane, first_row, kwin):
    rows = LAT_SEQ // GRID_W
    wh = min(NA_WIN_H, rows)
    for a in range(tq // GRID_W):
        rq = qi * (tq // GRID_W) + a
        r0 = jnp.clip(rq - wh // 2, 0, rows - wh)
        for kp in range(kwin // LANES):
            rk = first_row + 2 * kp
            ok_lo = ((rk >= r0) & (rk < r0 + wh)).astype(jnp.int32)
            ok_hi = ((rk + 1 >= r0) & (rk + 1 < r0 + wh)).astype(jnp.int32)
            ok = jnp.where(lane < GRID_W, ok_lo, ok_hi) > 0
            idx = jnp.clip(rk - rq + NA_WIN_H, 0, 2 * NA_WIN_H - 1)
            for hh in range(2 * pp):
                tile = jnp.where(ok, tab_ref[hh, idx], NEG_INF)
                out_ref[hh // 2, hh % 2, a * GRID_W:(a + 1) * GRID_W, kp * LANES:(kp + 1) * LANES] = tile


def _attention(q, k, v, *, n_batch, seq, row0, tq, pp, q_col0=0, k_col0=0, v_col0=0, kv_div=1,
               cache_k=None, cache_v=None, bias=None, sink=None, band=False, kwin=None,
               out_prev=None, name):
    kwin = seq if kwin is None else kwin
    n_groups = NA_HEADS * HEAD_DIM // (LANES * pp)
    nqt = seq // tq
    qw = LANES * pp
    kw = qw // kv_div
    has_cache, has_bias, has_sink = cache_k is not None, bias is not None, sink is not None
    aliased = out_prev is not None

    in_specs = [
        pl.BlockSpec((tq, qw), lambda g, qi, b: (row0 // tq + b * nqt + qi, q_col0 + g)),
        pl.BlockSpec((seq, kw), lambda g, qi, b: (row0 // seq + b, k_col0 + g)),
        pl.BlockSpec((seq, kw), lambda g, qi, b: (row0 // seq + b, v_col0 + g)),
    ]
    args = [q, k, v]
    if has_cache:
        cspec = pl.BlockSpec((None, PAST_LEN, kw), lambda g, qi, b: (b, 0, g))
        in_specs += [cspec, cspec]
        args += [cache_k, cache_v]
    if has_bias:
        in_specs.append(pl.BlockSpec((2 * pp, 2 * NA_WIN_H, GRID_W, LANES), lambda g, qi, b: (g, 0, 0, 0)))
        args.append(bias)
    if has_sink:
        in_specs.append(pl.BlockSpec(memory_space=pltpu.SMEM))
        args.append(sink)
    if aliased:
        in_specs.append(pl.BlockSpec(memory_space=pl.ANY))
        args.append(out_prev)
    return pl.pallas_call(
        functools.partial(_attn_kernel, tq=tq, tk=seq, kwin=kwin, pp=pp, kv_div=kv_div, has_cache=has_cache,
                          has_bias=has_bias, has_sink=has_sink, band=band, aliased=aliased),
        out_shape=jax.ShapeDtypeStruct((T_ALL, NA_HEADS * HEAD_DIM), BF16),
        grid=(n_groups, nqt, n_batch),
        in_specs=in_specs,
        out_specs=pl.BlockSpec((tq, qw), lambda g, qi, b: (row0 // tq + b * nqt + qi, g)),
        input_output_aliases={len(args) - 1: 0} if aliased else {},
        scratch_shapes=[pltpu.VMEM((pp, 2, tq, kwin), F32)] if has_bias else [],
        compiler_params=_params(),
        name=name,
    )(*args)


def _fourier_kernel(*refs, aliased, gps):
    u_ref, cc_ref, sc_ref, cl_ref, sl_ref = refs[:5]
    o_ref = refs[5 + int(aliased)]
    cc_b, sc_b, cl_b, sl_b = refs[6 + int(aliased):]

    @pl.when((pl.program_id(0) == 0) & (pl.program_id(1) == 0))
    def _():
        cc_b[...] = cc_ref[...].astype(BF16)
        sc_b[...] = sc_ref[...].astype(BF16)
        cl_b[...] = cl_ref[...].astype(BF16)
        sl_b[...] = sl_ref[...].astype(BF16)

    for g in range(gps):
        cols = slice(g * FT_GROUP_DIM, (g + 1) * FT_GROUP_DIM)
        u = u_ref[:, cols].astype(BF16)
        a = jnp.dot(u, cc_b[...], preferred_element_type=F32).astype(BF16)
        b = jnp.dot(u, sc_b[...], preferred_element_type=F32).astype(BF16)
        o = (jnp.dot(cl_b[...], a, preferred_element_type=F32)
             + jnp.dot(sl_b[...], b, preferred_element_type=F32))
        o_ref[:, cols] = o.astype(o_ref.dtype)


def _fourier(z, cc, sc, cl, sl, *, n_batch, seq, row0, gps, out_prev=None, name):
    aliased = out_prev is not None
    gd = FT_GROUP_DIM
    in_specs = [
        pl.BlockSpec((seq, gps * gd), lambda b, g: (row0 // seq + b, Z_UB // (gps * gd) + g)),
        pl.BlockSpec((gd, gd), lambda b, g: (0, 0)),
        pl.BlockSpec((gd, gd), lambda b, g: (0, 0)),
        pl.BlockSpec((seq, seq), lambda b, g: (0, 0)),
        pl.BlockSpec((seq, seq), lambda b, g: (0, 0)),
    ]
    args = [z, cc, sc, cl, sl]
    if aliased:
        in_specs.append(pl.BlockSpec(memory_space=pl.ANY))
        args.append(out_prev)
    return pl.pallas_call(
        functools.partial(_fourier_kernel, aliased=aliased, gps=gps),
        out_shape=jax.ShapeDtypeStruct((T_ALL, FT_GROUPS * gd), BF16),
        grid=(n_batch, FT_GROUPS // gps),
        in_specs=in_specs,
        out_specs=pl.BlockSpec((seq, gps * gd), lambda b, g: (row0 // seq + b, g)),
        input_output_aliases={len(args) - 1: 0} if aliased else {},
        scratch_shapes=[pltpu.VMEM((gd, gd), BF16), pltpu.VMEM((gd, gd), BF16),
                        pltpu.VMEM((seq, seq), BF16), pltpu.VMEM((seq, seq), BF16)],
        compiler_params=_params(dimension_semantics=("arbitrary", "arbitrary")),
        name=name,
    )(*args)


def _merge1_kernel(oa, ob, oc, wa, wb, wc, ga, gb, gc, m_ref):
    def branch(o, w, g):
        return jax.nn.sigmoid(g[...]) * jnp.dot(o[...], w[...].astype(BF16),
                                                preferred_element_type=F32)
    m_ref[...] = (branch(oa, wa, ga) + branch(ob, wb, gb) + branch(oc, wc, gc)).astype(m_ref.dtype)


def _merge1(o_a, o_b, o_c, w_a, w_b, w_c, z, layer):
    tm, tn = 1024, 512
    kdim = NA_HEADS * HEAD_DIM
    o_spec = pl.BlockSpec((tm, kdim), lambda i, j: (i, 0))
    w_spec = pl.BlockSpec((None, kdim, tn), lambda i, j: (layer, 0, j))

    def gate_spec(br):
        return pl.BlockSpec((tm, tn), lambda i, j: (i, (Z_GATE + br * D_MODEL) // tn + j))

    return pl.pallas_call(
        _merge1_kernel,
        out_shape=jax.ShapeDtypeStruct((T_ALL, D_MODEL), BF16),
        grid=(T_ALL // tm, D_MODEL // tn),
        in_specs=[o_spec, o_spec, o_spec, w_spec, w_spec, w_spec,
                  gate_spec(0), gate_spec(1), gate_spec(2)],
        out_specs=pl.BlockSpec((tm, tn), lambda i, j: (i, j)),
        compiler_params=_params(),
        name="merge_branches",
    )(o_a, o_b, o_c, w_a, w_b, w_c, z, z, z)


def _merge2_kernel(m_ref, w_ref, g_ref, *refs, tm):
    o_ref = refs[-1]
    x = _token_tile(refs[:-1], tm)
    o_ref[...] = x + g_ref[...] * jnp.dot(m_ref[...], w_ref[...].astype(BF16),
                                          preferred_element_type=F32)


def _merge2(m, w_o, x, mods, layer):
    tm, tn = 1024, 512
    x_specs, x_args = _token_specs(x, tm, tn)
    return pl.pallas_call(
        functools.partial(_merge2_kernel, tm=tm),
        out_shape=jax.ShapeDtypeStruct((T_ALL, D_MODEL), F32),
        grid=(T_ALL // tm, D_MODEL // tn),
        in_specs=[
            pl.BlockSpec((tm, D_MODEL), lambda i, j: (i, 0)),
            pl.BlockSpec((None, D_MODEL, tn), lambda i, j: (layer, 0, j)),
            pl.BlockSpec((None, 1, tn), lambda i, j: (_mod_row(i * tm), 0, 2 * D_MODEL // tn + j)),
        ] + x_specs,
        out_specs=pl.BlockSpec((tm, tn), lambda i, j: (i, j)),
        compiler_params=_params(),
        name="out_proj_residual",
    )(m, w_o, mods, *x_args)


def _router_kernel(x_ref, g_ref, sh_ref, sc_ref, wr_ref, br_ref, tri_ref,
                   h_ref, e_ref, gate_ref, rank_ref, cnt_ref, carry):
    @pl.when(pl.program_id(0) == 0)
    def _():
        carry[...] = jnp.zeros_like(carry)

    h = _rms_mod(x_ref[...], g_ref[...], sh_ref[...], sc_ref[...])
    h_ref[...] = h
    tm = h.shape[0]
    logits = _dot3(wr_ref[...], h, _NT) + br_ref[...]
    eidx = lax.broadcasted_iota(jnp.int32, (N_EXPERTS, tm), 0).astype(F32)
    work = logits
    vals, hots = [], []
    for kk in range(TOP_K):
        m = jnp.max(work, axis=0, keepdims=True)
        idx = jnp.min(jnp.where(work == m, eidx, float(N_EXPERTS)), axis=0, keepdims=True)
        hot = eidx == idx
        work = jnp.where(hot, -jnp.inf, work)
        vals.append(m)
        hots.append(hot)
        e_ref[kk:kk + 1, :] = idx.astype(jnp.int32)
    ex = [jnp.exp(vv - vals[0]) for vv in vals]
    den = ex[0] + ex[1] + ex[2] + ex[3]
    for kk in range(TOP_K):
        gate_ref[kk:kk + 1, :] = ex[kk] / den
    chosen = jnp.zeros((N_EXPERTS, tm), F32)
    for hot in hots:
        chosen = chosen + jnp.where(hot, 1.0, 0.0)
    before = jnp.dot(chosen.astype(BF16), tri_ref[...], preferred_element_type=F32) + carry[...]
    for kk in range(TOP_K):
        rk = jnp.sum(jnp.where(hots[kk], before, 0.0), axis=0, keepdims=True)
        rank_ref[kk:kk + 1, :] = rk.astype(jnp.int32)
    carry[...] = carry[...] + jnp.sum(chosen, axis=1, keepdims=True)
    cnt_ref[...] = carry[...]


def _router(x, g, mods, w_router_t, b_router, tri):
    tm = ROUTER_TM
    return pl.pallas_call(
        _router_kernel,
        out_shape=(
            jax.ShapeDtypeStruct((T_ALL, D_MODEL), F32),
            jax.ShapeDtypeStruct((TOP_K, T_ALL), jnp.int32),
            jax.ShapeDtypeStruct((TOP_K, T_ALL), F32),
            jax.ShapeDtypeStruct((TOP_K, T_ALL), jnp.int32),
            jax.ShapeDtypeStruct((N_EXPERTS, 1), F32),
        ),
        grid=(T_ALL // tm,),
        in_specs=[
            pl.BlockSpec((tm, D_MODEL), lambda i: (i, 0)),
            pl.BlockSpec((1, D_MODEL), lambda i: (0, 0)),
            pl.BlockSpec((None, 1, D_MODEL), lambda i: (_mod_row(i * tm), 0, 3)),
            pl.BlockSpec((None, 1, D_MODEL), lambda i: (_mod_row(i * tm), 0, 4)),
            pl.BlockSpec((N_EXPERTS, D_MODEL), lambda i: (0, 0)),
            pl.BlockSpec((N_EXPERTS, 1), lambda i: (0, 0)),
            pl.BlockSpec((tm, tm), lambda i: (0, 0)),
        ],
        out_specs=(
            pl.BlockSpec((tm, D_MODEL), lambda i: (i, 0)),
            pl.BlockSpec((TOP_K, tm), lambda i: (0, i)),
            pl.BlockSpec((TOP_K, tm), lambda i: (0, i)),
            pl.BlockSpec((TOP_K, tm), lambda i: (0, i)),
            pl.BlockSpec((N_EXPERTS, 1), lambda i: (0, 0)),
        ),
        scratch_shapes=[pltpu.VMEM((N_EXPERTS, 1), F32)],
        compiler_params=_params(dimension_semantics=("arbitrary",)),
        name="norm2_router",
    )(x, g.reshape(1, D_MODEL), mods, mods, w_router_t, b_router.reshape(N_EXPERTS, 1), tri)


def _moe_kernel(ue_ref, urow_ref, un_ref, uf_ref, ua_ref, src_ref,
                h_hbm, wg_ref, wl_ref, bg_ref, bl_ref, wd_ref, bd_ref, out_hbm,
                xf, xb, acc2, gsem, osem):
    u = pl.program_id(0)
    c = pl.program_id(1)
    nrows = un_ref[u]
    nfill = uf_ref[u]
    row0 = urow_ref[u]
    row0_next = urow_ref[jnp.minimum(u + 1, MOE_UNITS - 1)]

    def row_copy(r, tok):
        return pltpu.make_async_copy(h_hbm.at[pl.ds(tok, 1)], xf.at[pl.ds(r, 1)], gsem)

    def gather_sub(first_row, first_slot):
        for r in range(MOE_SUB):
            row_copy(first_slot + r, src_ref[first_row + r]).start()

    slot = u % 2
    acc = acc2.at[slot]

    def out_copy(which, first_row, sub):
        rows = pl.ds(sub * MOE_SUB, MOE_SUB)
        dst = pl.ds(pl.multiple_of(first_row + sub * MOE_SUB, MOE_SUB), MOE_SUB)
        return pltpu.make_async_copy(acc2.at[which, rows], out_hbm.at[dst], osem)

    @pl.when(c == 0)
    def _():
        @pl.when(u == 0)
        def _():
            for sub in range(MOE_NSUB):
                gather_sub(row0 + sub * MOE_SUB, sub * MOE_SUB)

        @pl.when(nrows > 0)
        def _():
            acc[...] = jnp.broadcast_to(bd_ref[...], acc.shape)

        @pl.when((u == 0) | (ua_ref[jnp.maximum(u - 1, 0)] > 0))
        def _():
            for r in range(MOE_UNIT):
                row_copy(r, 0).wait()

        for sub in range(MOE_NSUB):
            @pl.when(sub * MOE_SUB < nrows)
            def _():
                rows = pl.ds(sub * MOE_SUB, MOE_SUB)
                xb[rows, :] = xf[rows, :].astype(BF16)

    def expert_rows(sub, wg, wl, wd):
        rows = pl.ds(sub * MOE_SUB, MOE_SUB)
        xs = xb[rows, :]
        glu = jnp.dot(xs, wg, preferred_element_type=F32) + bg_ref[...]
        lin = jnp.dot(xs, wl, preferred_element_type=F32) + bl_ref[...]
        glu = jnp.minimum(glu, SWIGLU_LIMIT)
        lin = jnp.clip(lin, -SWIGLU_LIMIT, SWIGLU_LIMIT)
        hdn = glu * jax.nn.sigmoid(SWIGLU_ALPHA * glu) * (lin + 1.0)
        acc[rows, :] = acc[rows, :] + jnp.dot(hdn.astype(BF16), wd, preferred_element_type=F32)

    nsub = nrows // MOE_SUB
    for n in range(1, MOE_NSUB + 1):
        @pl.when(nsub == n)
        def _():
            wg = wg_ref[...].astype(BF16)
            wl = wl_ref[...].astype(BF16)
            wd = wd_ref[...].astype(BF16)
            gather_sub(row0_next + c * MOE_SUB, c * MOE_SUB)
            for sub in range(n):
                expert_rows(sub, wg, wl, wd)

    @pl.when(c == MOE_NCHUNK - 1)
    def _():
        @pl.when(nfill > 0)
        def _():
            acc[...] = jnp.zeros(acc.shape, F32)

        prev = jnp.maximum(u - 1, 0)
        nout_prev = jnp.where(u > 0, un_ref[prev] + uf_ref[prev], 0)
        for sub in range(MOE_NSUB):
            @pl.when(sub * MOE_SUB < nout_prev)
            def _():
                out_copy(1 - slot, urow_ref[prev], sub).wait()

        nout = nrows + nfill
        for sub in range(MOE_NSUB):
            @pl.when(sub * MOE_SUB < nout)
            def _():
                out_copy(slot, row0, sub).start()

        @pl.when(u == MOE_UNITS - 1)
        def _():
            for sub in range(MOE_NSUB):
                @pl.when(sub * MOE_SUB < nout)
                def _():
                    out_copy(slot, row0, sub).wait()


def _moe(h2, plan, w_gu, b_gu, w_down, b_down, layer):
    fc = MOE_FCHUNK

    def chunk(c, act, u):
        return jnp.where(act[u] > 0, c, MOE_NCHUNK - 1)

    def wg_map(u, c, ue, ur, un, uf, ua, sr):
        return (layer, ue[u], 0, chunk(c, ua, u))

    def wl_map(u, c, ue, ur, un, uf, ua, sr):
        return (layer, ue[u], 0, MOE_NCHUNK + chunk(c, ua, u))

    def wd_map(u, c, ue, ur, un, uf, ua, sr):
        return (layer, ue[u], chunk(c, ua, u), 0)

    def bd_map(u, c, ue, ur, un, uf, ua, sr):
        return (layer, ue[u], 0, 0)

    grid_spec = pltpu.PrefetchScalarGridSpec(
        num_scalar_prefetch=6,
        grid=(MOE_UNITS, MOE_NCHUNK),
        in_specs=[
            pl.BlockSpec(memory_space=pl.ANY),
            pl.BlockSpec((None, None, D_MODEL, fc), wg_map),
            pl.BlockSpec((None, None, D_MODEL, fc), wl_map),
            pl.BlockSpec((None, None, 1, fc), wg_map),
            pl.BlockSpec((None, None, 1, fc), wl_map),
            pl.BlockSpec((None, None, fc, D_MODEL), wd_map),
            pl.BlockSpec((None, None, 1, D_MODEL), bd_map),
        ],
        out_specs=pl.BlockSpec(memory_space=pl.ANY),
        scratch_shapes=[
            pltpu.VMEM((MOE_UNIT, D_MODEL), F32),
            pltpu.VMEM((MOE_UNIT, D_MODEL), BF16),
            pltpu.VMEM((2, MOE_UNIT, D_MODEL), F32),
            pltpu.SemaphoreType.DMA,
            pltpu.SemaphoreType.DMA,
        ],
    )
    return pl.pallas_call(
        _moe_kernel,
        out_shape=jax.ShapeDtypeStruct((MOE_ROWS, D_MODEL), F32),
        grid_spec=grid_spec,
        compiler_params=pltpu.CompilerParams(vmem_limit_bytes=MOE_VMEM_LIMIT,
                                             dimension_semantics=("arbitrary", "arbitrary")),
        name="moe_experts",
    )(*plan, h2, w_gu, w_gu,
      b_gu.reshape(DEPTH, N_EXPERTS, 1, 2 * D_FF), b_gu.reshape(DEPTH, N_EXPERTS, 1, 2 * D_FF),
      w_down, b_down.reshape(DEPTH, N_EXPERTS, 1, D_MODEL))


def _combine_kernel(dest_ref, rows_hbm, gate_ref, x_ref, g2_ref, *rest, split):
    o_refs, (buf, sem) = rest[:-2], rest[-2:]
    tm = COMBINE_TM
    base = pl.program_id(0) * tm

    def row_copy(r, kk, d):
        return pltpu.make_async_copy(rows_hbm.at[pl.ds(d, 1)], buf.at[kk, pl.ds(r, 1)], sem)

    for r in range(tm):
        for kk in range(TOP_K):
            row_copy(r, kk, dest_ref[(base + r) * TOP_K + kk]).start()
    for r in range(tm):
        for kk in range(TOP_K):
            row_copy(r, kk, 0).wait()
    gates = gate_ref[...]
    y = gates[:, 0:1] * buf[0]
    for kk in range(1, TOP_K):
        y = y + gates[:, kk:kk + 1] * buf[kk]
    val = x_ref[...] + g2_ref[...] * y
    if not split:
        o_refs[0][...] = val
    else:
        @pl.when(base < T_CTX)
        def _():
            o_refs[0][...] = val

        @pl.when(base >= T_CTX)
        def _():
            o_refs[1][...] = val


def _combine(dest_flat, rows, gates_t, x, mods, split):
    tm = COMBINE_TM
    n_ctx = T_CTX // tm
    if split:
        out_shape = (jax.ShapeDtypeStruct((T_CTX, D_MODEL), F32),
                     jax.ShapeDtypeStruct((T_LAT, D_MODEL), F32))
        out_specs = (pl.BlockSpec((tm, D_MODEL), lambda i, d: (jnp.minimum(i, n_ctx - 1), 0)),
                     pl.BlockSpec((tm, D_MODEL), lambda i, d: (jnp.maximum(i - n_ctx, 0), 0)))
    else:
        out_shape = jax.ShapeDtypeStruct((T_ALL, D_MODEL), F32)
        out_specs = pl.BlockSpec((tm, D_MODEL), lambda i, d: (i, 0))
    grid_spec = pltpu.PrefetchScalarGridSpec(
        num_scalar_prefetch=1,
        grid=(T_ALL // tm,),
        in_specs=[
            pl.BlockSpec(memory_space=pl.ANY),
            pl.BlockSpec((tm, TOP_K), lambda i, d: (i, 0)),
            pl.BlockSpec((tm, D_MODEL), lambda i, d: (i, 0)),
            pl.BlockSpec((None, 1, D_MODEL), lambda i, d: (_mod_row(i * tm), 0, 5)),
        ],
        out_specs=out_specs,
        scratch_shapes=[pltpu.VMEM((TOP_K, tm, D_MODEL), F32), pltpu.SemaphoreType.DMA],
    )
    return pl.pallas_call(
        functools.partial(_combine_kernel, split=split),
        out_shape=out_shape,
        grid_spec=grid_spec,
        compiler_params=_params(dimension_semantics=("arbitrary",)),
        name="moe_combine",
    )(dest_flat, rows, gates_t, x, mods)


def _src_kernel(dest_ref, src_ref):
    step = 8

    def clear(i, carry):
        for j in range(step):
            src_ref[i * step + j] = 0
        return carry

    def put(i, carry):
        for j in range(step):
            src_ref[dest_ref[i * step + j]] = i * (step // TOP_K) + j // TOP_K
        return carry

    lax.fori_loop(0, MOE_SRC_LEN // step, clear, 0)
    lax.fori_loop(0, N_ASSIGN // step, put, 0)


def _moe_src(dest_flat):
    return pl.pallas_call(
        _src_kernel,
        out_shape=jax.ShapeDtypeStruct((MOE_SRC_LEN,), jnp.int32),
        in_specs=[pl.BlockSpec(memory_space=pltpu.SMEM)],
        out_specs=pl.BlockSpec(memory_space=pltpu.SMEM),
        name="moe_src_rows",
    )(dest_flat)


def _moe_plan(e4, r4, counts):
    cnt = counts[:, 0].astype(jnp.int32)
    pad = (cnt + MOE_SUB - 1) // MOE_SUB * MOE_SUB
    pad_end = jnp.cumsum(pad)
    pad_start = pad_end - pad
    hot = e4[:, :, None] == jnp.arange(N_EXPERTS, dtype=jnp.int32)
    dest = jnp.sum(jnp.where(hot, pad_start, 0), axis=-1) + r4
    n_units_e = (pad + MOE_UNIT - 1) // MOE_UNIT
    unit_end = jnp.cumsum(n_units_e)
    unit_start = unit_end - n_units_e
    n_units = unit_end[-1]
    uid = jnp.minimum(jnp.arange(MOE_UNITS, dtype=jnp.int32), n_units - 1)
    unit_e = jnp.minimum(jnp.sum((uid[:, None] >= unit_end[None, :]).astype(jnp.int32), axis=1),
                         N_EXPERTS - 1)
    mine = unit_e[:, None] == jnp.arange(N_EXPERTS, dtype=jnp.int32)
    pick = lambda tab: jnp.sum(jnp.where(mine, tab, 0), axis=1)
    local = uid - pick(unit_start)
    unit_row = pick(pad_start) + local * MOE_UNIT
    unit_act = (jnp.arange(MOE_UNITS, dtype=jnp.int32) < n_units).astype(jnp.int32)
    unit_n = jnp.clip(pick(pad) - local * MOE_UNIT, 0, MOE_UNIT) * unit_act
    fill_row = pad_end[-1] + (jnp.arange(MOE_UNITS, dtype=jnp.int32) - n_units) * MOE_UNIT
    unit_fill = jnp.clip(MOE_ROWS - fill_row, 0, MOE_UNIT) * (1 - unit_act)
    unit_row = jnp.where(unit_act > 0, unit_row, jnp.minimum(fill_row, MOE_ROWS - MOE_SUB))
    dest_flat = dest.T.reshape(-1)
    i32 = lambda a: a.astype(jnp.int32)
    return dest_flat, (unit_e, i32(unit_row), i32(unit_n), i32(unit_fill), unit_act,
                       _moe_src(dest_flat))


def _dft_tables(n):
    idx = np.outer(np.arange(n), np.arange(n)) % n
    ang = 2.0 * np.pi * idx / n
    s = 1.0 / np.sqrt(n)
    return np.cos(ang) * s, np.sin(ang) * s


def _rope_tables():
    t = np.arange(LAT_SEQ)
    nf = HEAD_DIM // 4
    inv = ROPE_BASE ** (-np.arange(nf, dtype=np.float64) / nf)
    ang = np.concatenate([(t // GRID_W)[:, None] * inv, (t % GRID_W)[:, None] * inv], axis=-1)
    lane = np.arange(LANES)
    cos = np.cos(ang)[:, lane % (HEAD_DIM // 2)]
    sin = np.sin(ang)[:, lane % (HEAD_DIM // 2)]
    sign = np.where((lane % HEAD_DIM) < HEAD_DIM // 2, -1.0, 1.0)
    return cos.astype(np.float32), (sin * sign).astype(np.float32)


def _na_bias(rpb):
    c = np.arange(GRID_W)
    c0 = np.clip(c - NA_WIN_W // 2, 0, GRID_W - NA_WIN_W)
    c_ok = (c[None, :] >= c0[:, None]) & (c[None, :] < c0[:, None] + NA_WIN_W)
    dc = c[None, :] - c[:, None] + NA_WIN_W - 1
    pick = (np.arange(2 * NA_WIN_W - 1)[:, None, None] == dc[None]) & c_ok[None]
    cols = jnp.einsum('hrd,dqk->hrqk', rpb.astype(F32), jnp.asarray(pick, F32),
                      precision=lax.Precision.HIGHEST)
    cols = jnp.where(jnp.asarray(c_ok)[None, None], cols, NEG_INF)
    edge = jnp.full((NA_HEADS, 1, GRID_W, GRID_W), NEG_INF, F32)
    return jnp.concatenate([jnp.concatenate([edge, cols], axis=1),
                            jnp.concatenate([cols, edge], axis=1)], axis=-1)


def _dup_heads(cache):
    b, p = cache.shape[:2]
    return jnp.broadcast_to(cache[:, :, :, None, :], (b, p, SW_KV_HEADS, 2, HEAD_DIM)).reshape(
        b, p, SW_KV_HEADS * LANES)


def kernel(x_prompt, x_sample, cache_na_k, cache_na_v, cache_sw_k, cache_sw_v, c, c_ctx,
           w_ada, b_ada, norm1_g, norm2_g, w_in, qn_na, kn_na, qn_sw, kn_sw, rpb_na, sink_sw,
           w_br_na, w_br_ft, w_br_sw, w_o, w_router, b_router, w_gu, b_gu, w_down, b_down):
    nq = NA_HEADS * HEAD_DIM
    x = (x_prompt.reshape(T_CTX, D_MODEL), x_sample.reshape(T_LAT, D_MODEL))
    cvecs = jnp.zeros((MOD_ROWS, D_MODEL), F32).at[:LAT_BATCH].set(c).at[CTX_MOD_ROW].set(c_ctx)
    mods_all = _ada(cvecs, w_ada, b_ada)

    cc, sc = _dft_tables(FT_GROUP_DIM)
    cl_ctx, sl_ctx = _dft_tables(CTX_SEQ)
    cl_lat, sl_lat = _dft_tables(LAT_SEQ)
    as_bf16 = lambda a: jnp.asarray(a, BF16)
    cos_t, sin_t = _rope_tables()
    bd = as_bf16(np.kron(np.eye(LANES // HEAD_DIM), np.ones((HEAD_DIM, HEAD_DIM))))
    tri = as_bf16(np.triu(np.ones((ROUTER_TM, ROUTER_TM)), 1))
    tile2 = lambda g: jnp.tile(g.reshape(1, HEAD_DIM), (1, LANES // HEAD_DIM))

    caches = [jnp.zeros((CTX_BATCH, DEPTH, CTX_SEQ, w), F32)
              for w in (nq, nq, SW_KV_HEADS * HEAD_DIM, SW_KV_HEADS * HEAD_DIM)]
    for l in range(DEPTH):
        mods = mods_all[l].reshape(MOD_ROWS, 1, N_MOD)
        h = _prenorm(x, norm1_g[l], mods)
        z = _in_proj(h, w_in, l)
        qa, ka, qc, kc_dup, vc_dup, *caches = _prep(
            z, tile2(qn_na[l]), tile2(kn_na[l]), tile2(qn_sw[l]), tile2(kn_sw[l]), bd,
            jnp.asarray(cos_t), jnp.asarray(sin_t), caches, l)

        ctx = dict(n_batch=CTX_BATCH, seq=CTX_SEQ, row0=0, tq=CTX_SEQ, pp=8)
        lat = dict(n_batch=LAT_BATCH, seq=LAT_SEQ, row0=T_CTX, tq=256, pp=2)
        o_init = jnp.zeros((T_ALL, nq), BF16)
        o_a = _attention(qa, ka, z, v_col0=Z_VA // (LANES * ctx['pp']), out_prev=o_init,
                         name="attn_na_ctx", **ctx)
        o_a = _attention(qa, ka, z, v_col0=Z_VA // (LANES * lat['pp']),
                         cache_k=cache_na_k[:, l].reshape(LAT_BATCH, PAST_LEN, nq),
                         cache_v=cache_na_v[:, l].reshape(LAT_BATCH, PAST_LEN, nq),
                         bias=_na_bias(rpb_na[l]), kwin=NA_KEY_ROWS * GRID_W, out_prev=o_a,
                         name="attn_na_lat", **lat)
        o_c = _attention(qc, kc_dup, vc_dup, kv_div=2, sink=sink_sw[l], out_prev=o_init,
                         name="attn_sw_ctx", **ctx)
        o_c = _attention(qc, kc_dup, vc_dup, kv_div=2, sink=sink_sw[l],
                         cache_k=_dup_heads(cache_sw_k[:, l]), cache_v=_dup_heads(cache_sw_v[:, l]),
                         band=True, kwin=lat['tq'] + 2 * SW_WINDOW, out_prev=o_c, name="attn_sw_lat", **lat)
        as_f32 = lambda a: jnp.asarray(a, F32)
        o_b = _fourier(z, as_f32(cc), as_f32(-sc), as_f32(cl_ctx), as_f32(sl_ctx),
                       n_batch=CTX_BATCH, seq=CTX_SEQ, row0=0, gps=FT_GROUPS, out_prev=o_init,
                       name="fourier_ctx")
        o_b = _fourier(z, as_f32(cc), as_f32(-sc), as_f32(cl_lat), as_f32(sl_lat),
                       n_batch=LAT_BATCH, seq=LAT_SEQ, row0=T_CTX, gps=1, out_prev=o_b,
                       name="fourier_lat")
        m = _merge1(o_a, o_b, o_c, w_br_na, w_br_ft, w_br_sw, z, l)
        x = _merge2(m, w_o, x, mods, l)

        h2, e4, g4, r4, counts = _router(x, norm2_g[l], mods, w_router[l].T, b_router[l], tri)
        dest_flat, plan = _moe_plan(e4, r4, counts)
        rows = _moe(h2, plan, w_gu, b_gu, w_down, b_down, l)
        x = _combine(dest_flat, rows, g4.T, x, mods, split=(l == DEPTH - 1))

    y_prompt = x[0].reshape(CTX_BATCH, CTX_SEQ, D_MODEL)
    y_sample = x[1].reshape(LAT_BATCH, LAT_SEQ, D_MODEL)
    heads = lambda c, n: c.reshape(CTX_BATCH, DEPTH, CTX_SEQ, n, HEAD_DIM)
    return (y_prompt, y_sample, heads(caches[0], NA_HEADS), heads(caches[1], NA_HEADS),
            heads(caches[2], SW_KV_HEADS), heads(caches[3], SW_KV_HEADS))
```

```python
import functools

import numpy as np
import jax
import jax.numpy as jnp
from jax import lax
from jax.experimental import pallas as pl
from jax.experimental.pallas import tpu as pltpu

F32 = jnp.float32
BF16 = jnp.bfloat16

D_MODEL = 2048
DEPTH = 2
CTX_BATCH = 32
CTX_SEQ = 256
LAT_BATCH = 4
LAT_SEQ = 1024
PAST_LEN = 256
T_CTX = CTX_BATCH * CTX_SEQ
T_LAT = LAT_BATCH * LAT_SEQ
T_ALL = T_CTX + T_LAT
GRID_W = 64
HEAD_DIM = 64
NA_HEADS = 16
NA_WIN_H = 8
NA_WIN_W = 16
FT_GROUPS = 4
FT_GROUP_DIM = 256
SW_HEADS = 16
SW_KV_HEADS = 4
SW_WINDOW = 128
ROPE_BASE = 10000.0
N_EXPERTS = 32
TOP_K = 4
D_FF = 2048
SWIGLU_LIMIT = 7.0
SWIGLU_ALPHA = 1.702
EPS = 1e-6
NEG_INF = -1e30
NA_KEY_ROWS = 12

Z_QA, Z_KA, Z_VA, Z_UB, Z_QC, Z_KC, Z_VC, Z_GATE = 0, 1024, 2048, 3072, 4096, 5120, 5376, 5632
IN_WIDTH = Z_GATE + 3 * D_MODEL
N_MOD = 6 * D_MODEL
CTX_MOD_ROW = LAT_BATCH
MOD_ROWS = 8

LANES = 128
VMEM_LIMIT = 56 * 1024 * 1024
MOE_VMEM_LIMIT = 62 * 1024 * 1024

MOE_SUB = 256
MOE_UNIT = 1024
MOE_NSUB = MOE_UNIT // MOE_SUB
MOE_FCHUNK = 512
MOE_NCHUNK = D_FF // MOE_FCHUNK
N_ASSIGN = T_ALL * TOP_K
MOE_ROWS = -(-(N_ASSIGN + N_EXPERTS * (MOE_SUB - 1)) // MOE_UNIT) * MOE_UNIT
MOE_UNITS = MOE_ROWS // MOE_UNIT + N_EXPERTS
MOE_SRC_LEN = MOE_ROWS + MOE_UNIT
ROUTER_TM = 256
COMBINE_TM = 256


def _params(**kw):
    return pltpu.CompilerParams(vmem_limit_bytes=VMEM_LIMIT, **kw)


def _mod_row(row0):
    return jnp.where(row0 < T_CTX, CTX_MOD_ROW, (row0 - T_CTX) // LAT_SEQ)


def _split(a):
    hi = a.astype(BF16)
    lo = (a - hi.astype(F32)).astype(BF16)
    return hi, lo


def _dot3(a, b, dims):
    ah, al = _split(a)
    bh, bl = _split(b)
    d = functools.partial(lax.dot_general, dimension_numbers=dims, preferred_element_type=F32)
    return d(ah, bh) + d(ah, bl) + d(al, bh)


_NN = (((1,), (0,)), ((), ()))
_NT = (((1,), (1,)), ((), ()))


def _ada_kernel(c_ref, w_ref, b_ref, o_ref):
    c = c_ref[...]
    s = c * jax.nn.sigmoid(c)
    o_ref[...] = _dot3(s, w_ref[...], _NN) + b_ref[...]


def _ada(cvecs, w_ada, b_ada):
    tn = 1024
    return pl.pallas_call(
        _ada_kernel,
        out_shape=jax.ShapeDtypeStruct((DEPTH, MOD_ROWS, N_MOD), F32),
        grid=(DEPTH, N_MOD // tn),
        in_specs=[
            pl.BlockSpec((MOD_ROWS, D_MODEL), lambda l, j: (0, 0)),
            pl.BlockSpec((None, D_MODEL, tn), lambda l, j: (l, 0, j)),
            pl.BlockSpec((None, 1, tn), lambda l, j: (l, 0, j)),
        ],
        out_specs=pl.BlockSpec((None, MOD_ROWS, tn), lambda l, j: (l, 0, j)),
        compiler_params=_params(),
        name="ada_table",
    )(cvecs, w_ada, b_ada.reshape(DEPTH, 1, N_MOD))


def _rms_mod(x, g, shift, scale):
    y = x * lax.rsqrt(jnp.mean(x * x, axis=-1, keepdims=True) + EPS)
    return (y * g) * (1.0 + scale) + shift


def _token_specs(x, tm, tn):
    col = lambda j: j[0] if j else 0
    if not isinstance(x, tuple):
        return [pl.BlockSpec((tm, tn), lambda i, *j: (i, col(j)))], [x]
    n_ctx = T_CTX // tm
    return ([pl.BlockSpec((tm, tn), lambda i, *j: (jnp.minimum(i, n_ctx - 1),
                                                   jnp.where(i < n_ctx, col(j), 0))),
             pl.BlockSpec((tm, tn), lambda i, *j: (jnp.maximum(i - n_ctx, 0),
                                                   jnp.where(i >= n_ctx, col(j), 0)))], list(x))


def _token_tile(refs, tm):
    if len(refs) == 1:
        return refs[0][...]
    return jnp.where(pl.program_id(0) * tm >= T_CTX, refs[1][...], refs[0][...])


def _prenorm_kernel(*refs, tm):
    g_ref, sh_ref, sc_ref, h_ref = refs[-4:]
    x = _token_tile(refs[:-4], tm)
    h_ref[...] = _rms_mod(x, g_ref[...], sh_ref[...], sc_ref[...]).astype(h_ref.dtype)


def _prenorm(x, g, mods):
    tm = 512
    x_specs, x_args = _token_specs(x, tm, D_MODEL)
    return pl.pallas_call(
        functools.partial(_prenorm_kernel, tm=tm),
        out_shape=jax.ShapeDtypeStruct((T_ALL, D_MODEL), BF16),
        grid=(T_ALL // tm,),
        in_specs=x_specs + [
            pl.BlockSpec((1, D_MODEL), lambda i: (0, 0)),
            pl.BlockSpec((None, 1, D_MODEL), lambda i: (_mod_row(i * tm), 0, 0)),
            pl.BlockSpec((None, 1, D_MODEL), lambda i: (_mod_row(i * tm), 0, 1)),
        ],
        out_specs=pl.BlockSpec((tm, D_MODEL), lambda i: (i, 0)),
        compiler_params=_params(),
        name="prenorm",
    )(*x_args, g.reshape(1, D_MODEL), mods, mods)


def _mm_kernel(a_ref, w_ref, o_ref):
    o_ref[...] = jnp.dot(a_ref[...], w_ref[...].astype(BF16),
                         preferred_element_type=F32).astype(o_ref.dtype)


def _in_proj(h, w_in, layer):
    tm, tn = 2048, 512
    return pl.pallas_call(
        _mm_kernel,
        out_shape=jax.ShapeDtypeStruct((T_ALL, IN_WIDTH), F32),
        grid=(T_ALL // tm, IN_WIDTH // tn),
        in_specs=[
            pl.BlockSpec((tm, D_MODEL), lambda i, j: (i, 0)),
            pl.BlockSpec((None, D_MODEL, tn), lambda i, j: (layer, 0, j)),
        ],
        out_specs=pl.BlockSpec((tm, tn), lambda i, j: (i, j)),
        compiler_params=_params(),
        name="in_proj",
    )(h, w_in)


def _head_norm(x, gain, bd):
    hi, lo = _split(x * x)
    ss = jnp.dot(hi, bd, preferred_element_type=F32) + jnp.dot(lo, bd, preferred_element_type=F32)
    return x * lax.rsqrt(ss * (1.0 / HEAD_DIM) + EPS) * gain


def _prep_kernel(qa_ref, ka_ref, va_ref, qc_ref, kc_ref, vc_ref, gqa_ref, gka_ref, gqc_ref, gkc_ref,
                 bd_ref, cos_ref, sin_ref, nk_in, nv_in, sk_in, sv_in,
                 qa_o, ka_o, qc_o, kcd_o, vcd_o, nk_o, nv_o, sk_o, sv_o, nk_s, sk_s, *, tm):
    del nk_in, nv_in, sk_in, sv_in
    is_lat = pl.program_id(0) * tm >= T_CTX
    is_ctx = jnp.logical_not(is_lat)
    lane = lax.broadcasted_iota(jnp.int32, (1, LANES), 1)
    first_half = (lane % HEAD_DIM) < (HEAD_DIM // 2)
    low_head = lane < HEAD_DIM
    bd = bd_ref[...]
    cos = jnp.where(is_lat, cos_ref[...], 1.0)
    sin = jnp.where(is_lat, sin_ref[...], 0.0)

    def rope(y):
        partner = jnp.where(first_half, pltpu.roll(y, LANES - HEAD_DIM // 2, 1),
                            pltpu.roll(y, HEAD_DIM // 2, 1))
        return y * cos + partner * sin

    for s in range(NA_HEADS * HEAD_DIM // LANES):
        sl = slice(s * LANES, (s + 1) * LANES)
        qa_o[:, sl] = _head_norm(qa_ref[:, sl], gqa_ref[...], bd).astype(qa_o.dtype)
        kan = _head_norm(ka_ref[:, sl], gka_ref[...], bd)
        ka_o[:, sl] = kan.astype(ka_o.dtype)
        nk_s[:, sl] = kan
        qc_o[:, sl] = rope(_head_norm(qc_ref[:, sl], gqc_ref[...], bd)).astype(qc_o.dtype)
    for s in range(SW_KV_HEADS * HEAD_DIM // LANES):
        sl = slice(s * LANES, (s + 1) * LANES)
        kn = _head_norm(kc_ref[:, sl], gkc_ref[...], bd)
        sk_s[:, sl] = kn
        kr = rope(kn)
        v = vc_ref[:, sl]
        kr_sw = pltpu.roll(kr, HEAD_DIM, 1)
        v_sw = pltpu.roll(v, HEAD_DIM, 1)
        dt = kcd_o.dtype
        kcd_o[:, (2 * s) * LANES:(2 * s + 1) * LANES] = jnp.where(low_head, kr, kr_sw).astype(dt)
        kcd_o[:, (2 * s + 1) * LANES:(2 * s + 2) * LANES] = jnp.where(low_head, kr_sw, kr).astype(dt)
        vcd_o[:, (2 * s) * LANES:(2 * s + 1) * LANES] = jnp.where(low_head, v, v_sw).astype(dt)
        vcd_o[:, (2 * s + 1) * LANES:(2 * s + 2) * LANES] = jnp.where(low_head, v_sw, v).astype(dt)

    @pl.when(is_ctx)
    def _():
        nk_o[...] = nk_s[...]
        nv_o[...] = va_ref[...]
        sk_o[...] = sk_s[...]
        sv_o[...] = vc_ref[...]


def _prep(z, gqa, gka, gqc, gkc, bd, cos_t, sin_t, caches, layer):
    tm = 256
    nq = NA_HEADS * HEAD_DIM
    nkv = SW_KV_HEADS * HEAD_DIM
    lat_tiles = LAT_SEQ // tm
    n_ctx = T_CTX // tm
    assert tm == CTX_SEQ

    def pos_map(i):
        return (jnp.maximum(i - n_ctx, 0) % lat_tiles, 0)

    def cache_spec(width):
        return pl.BlockSpec((None, None, tm, width), lambda i: (jnp.minimum(i, n_ctx - 1), layer, 0, 0))

    gain = pl.BlockSpec((1, LANES), lambda i: (0, 0))
    alias = pl.BlockSpec(memory_space=pl.ANY)
    n_in = 13
    return pl.pallas_call(
        functools.partial(_prep_kernel, tm=tm),
        out_shape=(
            jax.ShapeDtypeStruct((T_ALL, nq), BF16),
            jax.ShapeDtypeStruct((T_ALL, nq), BF16),
            jax.ShapeDtypeStruct((T_ALL, nq), BF16),
            jax.ShapeDtypeStruct((T_ALL, 2 * nkv), BF16),
            jax.ShapeDtypeStruct((T_ALL, 2 * nkv), BF16),
        ) + tuple(jax.ShapeDtypeStruct(c.shape, c.dtype) for c in caches),
        grid=(T_ALL // tm,),
        in_specs=[
            pl.BlockSpec((tm, nq), lambda i: (i, Z_QA // nq)),
            pl.BlockSpec((tm, nq), lambda i: (i, Z_KA // nq)),
            pl.BlockSpec((tm, nq), lambda i: (jnp.minimum(i, n_ctx - 1), Z_VA // nq)),
            pl.BlockSpec((tm, nq), lambda i: (i, Z_QC // nq)),
            pl.BlockSpec((tm, nkv), lambda i: (i, Z_KC // nkv)),
            pl.BlockSpec((tm, nkv), lambda i: (i, Z_VC // nkv)),
            gain, gain, gain, gain,
            pl.BlockSpec((LANES, LANES), lambda i: (0, 0)),
            pl.BlockSpec((tm, LANES), pos_map),
            pl.BlockSpec((tm, LANES), pos_map),
            alias, alias, alias, alias,
        ],
        out_specs=(
            pl.BlockSpec((tm, nq), lambda i: (i, 0)),
            pl.BlockSpec((tm, nq), lambda i: (i, 0)),
            pl.BlockSpec((tm, nq), lambda i: (i, 0)),
            pl.BlockSpec((tm, 2 * nkv), lambda i: (i, 0)),
            pl.BlockSpec((tm, 2 * nkv), lambda i: (i, 0)),
            cache_spec(nq), cache_spec(nq), cache_spec(nkv), cache_spec(nkv),
        ),
        input_output_aliases={n_in + k: 5 + k for k in range(4)},
        scratch_shapes=[pltpu.VMEM((tm, nq), F32), pltpu.VMEM((tm, nkv), F32)],
        compiler_params=_params(),
        name="qk_prep",
    )(z, z, z, z, z, z, gqa, gka, gqc, gkc, bd, cos_t, sin_t, *caches)


def _attn_kernel(*refs, tq, tk, kwin, pp, kv_div, has_cache, has_bias, has_sink, band, aliased):
    it = iter(refs)
    q_ref, k_ref, v_ref = next(it), next(it), next(it)
    kc_ref = vc_ref = bias_ref = sink_ref = None
    if has_cache:
        kc_ref, vc_ref = next(it), next(it)
    if has_bias:
        bias_ref = next(it)
    if has_sink:
        sink_ref = next(it)
    if aliased:
        next(it)
    o_ref = next(it)
    bias_scr = next(it) if has_bias else None

    group = pl.program_id(0)
    qi = pl.program_id(1)
    lane = lax.broadcasted_iota(jnp.int32, (1, LANES), 1)
    krows = slice(None)
    if band:
        kstart = pl.multiple_of(jnp.clip(qi * tq - SW_WINDOW, 0, tk - kwin), LANES)
        krows = pl.ds(kstart, kwin)
        qpos = qi * tq + lax.broadcasted_iota(jnp.int32, (tq, 1), 0)
        kpos = kstart + lax.broadcasted_iota(jnp.int32, (1, kwin), 1)
        in_band = jnp.abs(qpos - kpos) <= SW_WINDOW
    if has_bias:
        first_row = jnp.clip(qi * (tq // GRID_W) - NA_WIN_H // 2, 0, (tk - kwin) // GRID_W)
        krows = pl.ds(pl.multiple_of(first_row * GRID_W, LANES), kwin)

        @pl.when(pl.program_id(2) == 0)
        def _():
            _na_bias_tile(bias_ref, bias_scr, qi, tq, pp, lane, first_row, kwin)

    for j in range(pp):
        qs = slice(j * LANES, (j + 1) * LANES)
        ks = slice((j // kv_div) * LANES, (j // kv_div + 1) * LANES)
        q = q_ref[:, qs] * (HEAD_DIM ** -0.5)
        k = k_ref[krows, ks].astype(BF16)
        v = v_ref[krows, ks]
        if has_cache:
            kc = kc_ref[:, ks].astype(BF16)
            vc = vc_ref[:, ks]
        acc = jnp.zeros((tq, LANES), F32)
        for half in range(2):
            hm = (lane < HEAD_DIM) if half == 0 else (lane >= HEAD_DIM)
            qm = jnp.where(hm, q, 0.0).astype(BF16)
            s = lax.dot_general(qm, k, _NT, preferred_element_type=F32)
            if has_bias:
                s = s + bias_scr[j, half]
            if band:
                s = jnp.where(in_band, s, NEG_INF)
            m = jnp.max(s, axis=1, keepdims=True)
            if has_cache:
                sc = lax.dot_general(qm, kc, _NT, preferred_element_type=F32)
                m = jnp.maximum(m, jnp.max(sc, axis=1, keepdims=True))
            if has_sink:
                sk = sink_ref[2 * (group * pp + j) + half]
                m = jnp.maximum(m, sk)
            p = jnp.exp(s - m)
            denom = jnp.sum(p, axis=1, keepdims=True)
            o = jnp.dot(p.astype(BF16), jnp.where(hm, v, 0.0).astype(BF16),
                        preferred_element_type=F32)
            if has_cache:
                pc = jnp.exp(sc - m)
                denom = denom + jnp.sum(pc, axis=1, keepdims=True)
                o = o + jnp.dot(pc.astype(BF16), jnp.where(hm, vc, 0.0).astype(BF16),
                                preferred_element_type=F32)
            if has_sink:
                denom = denom + jnp.exp(sk - m)
            acc = acc + o / denom
        o_ref[:, qs] = acc.astype(o_ref.dtype)


def _na_bias_tile(tab_ref, out_ref, qi, tq, pp, lane, first_row, kwin):
    rows = LAT_SEQ // GRID_W
    wh = min(NA_WIN_H, rows)
    for a in range(tq // GRID_W):
        rq = qi * (tq // GRID_W) + a
        r0 = jnp.clip(rq - wh // 2, 0, rows - wh)
        for kp in range(kwin // LANES):
            rk = first_row + 2 * kp
            ok_lo = ((rk >= r0) & (rk < r0 + wh)).astype(jnp.int32)
            ok_hi = ((rk + 1 >= r0) & (rk + 1 < r0 + wh)).astype(jnp.int32)
            ok = jnp.where(lane < GRID_W, ok_lo, ok_hi) > 0
            idx = jnp.clip(rk - rq + NA_WIN_H, 0, 2 * NA_WIN_H - 1)
            for hh in range(2 * pp):
                tile = jnp.where(ok, tab_ref[hh, idx], NEG_INF)
                out_ref[hh // 2, hh % 2, a * GRID_W:(a + 1) * GRID_W, kp * LANES:(kp + 1) * LANES] = tile


def _attention(q, k, v, *, n_batch, seq, row0, tq, pp, q_col0=0, k_col0=0, v_col0=0, kv_div=1,
               cache_k=None, cache_v=None, bias=None, sink=None, band=False, kwin=None,
               out_prev=None, name):
    kwin = seq if kwin is None else kwin
    n_groups = NA_HEADS * HEAD_DIM // (LANES * pp)
    nqt = seq // tq
    qw = LANES * pp
    kw = qw // kv_div
    has_cache, has_bias, has_sink = cache_k is not None, bias is not None, sink is not None
    aliased = out_prev is not None

    in_specs = [
        pl.BlockSpec((tq, qw), lambda g, qi, b: (row0 // tq + b * nqt + qi, q_col0 + g)),
        pl.BlockSpec((seq, kw), lambda g, qi, b: (row0 // seq + b, k_col0 + g)),
        pl.BlockSpec((seq, kw), lambda g, qi, b: (row0 // seq + b, v_col0 + g)),
    ]
    args = [q, k, v]
    if has_cache:
        cspec = pl.BlockSpec((None, PAST_LEN, kw), lambda g, qi, b: (b, 0, g))
        in_specs += [cspec, cspec]
        args += [cache_k, cache_v]
    if has_bias:
        in_specs.append(pl.BlockSpec((2 * pp, 2 * NA_WIN_H, GRID_W, LANES), lambda g, qi, b: (g, 0, 0, 0)))
        args.append(bias)
    if has_sink:
        in_specs.append(pl.BlockSpec(memory_space=pltpu.SMEM))
        args.append(sink)
    if aliased:
        in_specs.append(pl.BlockSpec(memory_space=pl.ANY))
        args.append(out_prev)
    return pl.pallas_call(
        functools.partial(_attn_kernel, tq=tq, tk=seq, kwin=kwin, pp=pp, kv_div=kv_div, has_cache=has_cache,
                          has_bias=has_bias, has_sink=has_sink, band=band, aliased=aliased),
        out_shape=jax.ShapeDtypeStruct((T_ALL, NA_HEADS * HEAD_DIM), BF16),
        grid=(n_groups, nqt, n_batch),
        in_specs=in_specs,
        out_specs=pl.BlockSpec((tq, qw), lambda g, qi, b: (row0 // tq + b * nqt + qi, g)),
        input_output_aliases={len(args) - 1: 0} if aliased else {},
        scratch_shapes=[pltpu.VMEM((pp, 2, tq, kwin), F32)] if has_bias else [],
        compiler_params=_params(),
        name=name,
    )(*args)


def _fourier_kernel(*refs, aliased, gps):
    u_ref, cc_ref, sc_ref, cl_ref, sl_ref = refs[:5]
    o_ref = refs[5 + int(aliased)]
    cc_b, sc_b, cl_b, sl_b = refs[6 + int(aliased):]

    @pl.when((pl.program_id(0) == 0) & (pl.program_id(1) == 0))
    def _():
        cc_b[...] = cc_ref[...].astype(BF16)
        sc_b[...] = sc_ref[...].astype(BF16)
        cl_b[...] = cl_ref[...].astype(BF16)
        sl_b[...] = sl_ref[...].astype(BF16)

    for g in range(gps):
        cols = slice(g * FT_GROUP_DIM, (g + 1) * FT_GROUP_DIM)
        u = u_ref[:, cols].astype(BF16)
        a = jnp.dot(u, cc_b[...], preferred_element_type=F32).astype(BF16)
        b = jnp.dot(u, sc_b[...], preferred_element_type=F32).astype(BF16)
        o = (jnp.dot(cl_b[...], a, preferred_element_type=F32)
             + jnp.dot(sl_b[...], b, preferred_element_type=F32))
        o_ref[:, cols] = o.astype(o_ref.dtype)


def _fourier(z, cc, sc, cl, sl, *, n_batch, seq, row0, gps, out_prev=None, name):
    aliased = out_prev is not None
    gd = FT_GROUP_DIM
    in_specs = [
        pl.BlockSpec((seq, gps * gd), lambda b, g: (row0 // seq + b, Z_UB // (gps * gd) + g)),
        pl.BlockSpec((gd, gd), lambda b, g: (0, 0)),
        pl.BlockSpec((gd, gd), lambda b, g: (0, 0)),
        pl.BlockSpec((seq, seq), lambda b, g: (0, 0)),
        pl.BlockSpec((seq, seq), lambda b, g: (0, 0)),
    ]
    args = [z, cc, sc, cl, sl]
    if aliased:
        in_specs.append(pl.BlockSpec(memory_space=pl.ANY))
        args.append(out_prev)
    return pl.pallas_call(
        functools.partial(_fourier_kernel, aliased=aliased, gps=gps),
        out_shape=jax.ShapeDtypeStruct((T_ALL, FT_GROUPS * gd), BF16),
        grid=(n_batch, FT_GROUPS // gps),
        in_specs=in_specs,
        out_specs=pl.BlockSpec((seq, gps * gd), lambda b, g: (row0 // seq + b, g)),
        input_output_aliases={len(args) - 1: 0} if aliased else {},
        scratch_shapes=[pltpu.VMEM((gd, gd), BF16), pltpu.VMEM((gd, gd), BF16),
                        pltpu.VMEM((seq, seq), BF16), pltpu.VMEM((seq, seq), BF16)],
        compiler_params=_params(dimension_semantics=("arbitrary", "arbitrary")),
        name=name,
    )(*args)


def _merge1_kernel(oa, ob, oc, wa, wb, wc, ga, gb, gc, m_ref):
    def branch(o, w, g):
        return jax.nn.sigmoid(g[...]) * jnp.dot(o[...], w[...].astype(BF16),
                                                preferred_element_type=F32)
    m_ref[...] = (branch(oa, wa, ga) + branch(ob, wb, gb) + branch(oc, wc, gc)).astype(m_ref.dtype)


def _merge1(o_a, o_b, o_c, w_a, w_b, w_c, z, layer):
    tm, tn = 1024, 512
    kdim = NA_HEADS * HEAD_DIM
    o_spec = pl.BlockSpec((tm, kdim), lambda i, j: (i, 0))
    w_spec = pl.BlockSpec((None, kdim, tn), lambda i, j: (layer, 0, j))

    def gate_spec(br):
        return pl.BlockSpec((tm, tn), lambda i, j: (i, (Z_GATE + br * D_MODEL) // tn + j))

    return pl.pallas_call(
        _merge1_kernel,
        out_shape=jax.ShapeDtypeStruct((T_ALL, D_MODEL), BF16),
        grid=(T_ALL // tm, D_MODEL // tn),
        in_specs=[o_spec, o_spec, o_spec, w_spec, w_spec, w_spec,
                  gate_spec(0), gate_spec(1), gate_spec(2)],
        out_specs=pl.BlockSpec((tm, tn), lambda i, j: (i, j)),
        compiler_params=_params(),
        name="merge_branches",
    )(o_a, o_b, o_c, w_a, w_b, w_c, z, z, z)


def _merge2_kernel(m_ref, w_ref, g_ref, *refs, tm):
    o_ref = refs[-1]
    x = _token_tile(refs[:-1], tm)
    o_ref[...] = x + g_ref[...] * jnp.dot(m_ref[...], w_ref[...].astype(BF16),
                                          preferred_element_type=F32)


def _merge2(m, w_o, x, mods, layer):
    tm, tn = 1024, 512
    x_specs, x_args = _token_specs(x, tm, tn)
    return pl.pallas_call(
        functools.partial(_merge2_kernel, tm=tm),
        out_shape=jax.ShapeDtypeStruct((T_ALL, D_MODEL), F32),
        grid=(T_ALL // tm, D_MODEL // tn),
        in_specs=[
            pl.BlockSpec((tm, D_MODEL), lambda i, j: (i, 0)),
            pl.BlockSpec((None, D_MODEL, tn), lambda i, j: (layer, 0, j)),
            pl.BlockSpec((None, 1, tn), lambda i, j: (_mod_row(i * tm), 0, 2 * D_MODEL // tn + j)),
        ] + x_specs,
        out_specs=pl.BlockSpec((tm, tn), lambda i, j: (i, j)),
        compiler_params=_params(),
        name="out_proj_residual",
    )(m, w_o, mods, *x_args)


def _router_kernel(x_ref, g_ref, sh_ref, sc_ref, wr_ref, br_ref, tri_ref,
                   h_ref, e_ref, gate_ref, rank_ref, cnt_ref, carry):
    @pl.when(pl.program_id(0) == 0)
    def _():
        carry[...] = jnp.zeros_like(carry)

    h = _rms_mod(x_ref[...], g_ref[...], sh_ref[...], sc_ref[...])
    h_ref[...] = h
    tm = h.shape[0]
    logits = _dot3(wr_ref[...], h, _NT) + br_ref[...]
    eidx = lax.broadcasted_iota(jnp.int32, (N_EXPERTS, tm), 0).astype(F32)
    work = logits
    vals, hots = [], []
    for kk in range(TOP_K):
        m = jnp.max(work, axis=0, keepdims=True)
        idx = jnp.min(jnp.where(work == m, eidx, float(N_EXPERTS)), axis=0, keepdims=True)
        hot = eidx == idx
        work = jnp.where(hot, -jnp.inf, work)
        vals.append(m)
        hots.append(hot)
        e_ref[kk:kk + 1, :] = idx.astype(jnp.int32)
    ex = [jnp.exp(vv - vals[0]) for vv in vals]
    den = ex[0] + ex[1] + ex[2] + ex[3]
    for kk in range(TOP_K):
        gate_ref[kk:kk + 1, :] = ex[kk] / den
    chosen = jnp.zeros((N_EXPERTS, tm), F32)
    for hot in hots:
        chosen = chosen + jnp.where(hot, 1.0, 0.0)
    before = jnp.dot(chosen.astype(BF16), tri_ref[...], preferred_element_type=F32) + carry[...]
    for kk in range(TOP_K):
        rk = jnp.sum(jnp.where(hots[kk], before, 0.0), axis=0, keepdims=True)
        rank_ref[kk:kk + 1, :] = rk.astype(jnp.int32)
    carry[...] = carry[...] + jnp.sum(chosen, axis=1, keepdims=True)
    cnt_ref[...] = carry[...]


def _router(x, g, mods, w_router_t, b_router, tri):
    tm = ROUTER_TM
    return pl.pallas_call(
        _router_kernel,
        out_shape=(
            jax.ShapeDtypeStruct((T_ALL, D_MODEL), F32),
            jax.ShapeDtypeStruct((TOP_K, T_ALL), jnp.int32),
            jax.ShapeDtypeStruct((TOP_K, T_ALL), F32),
            jax.ShapeDtypeStruct((TOP_K, T_ALL), jnp.int32),
            jax.ShapeDtypeStruct((N_EXPERTS, 1), F32),
        ),
        grid=(T_ALL // tm,),
        in_specs=[
            pl.BlockSpec((tm, D_MODEL), lambda i: (i, 0)),
            pl.BlockSpec((1, D_MODEL), lambda i: (0, 0)),
            pl.BlockSpec((None, 1, D_MODEL), lambda i: (_mod_row(i * tm), 0, 3)),
            pl.BlockSpec((None, 1, D_MODEL), lambda i: (_mod_row(i * tm), 0, 4)),
            pl.BlockSpec((N_EXPERTS, D_MODEL), lambda i: (0, 0)),
            pl.BlockSpec((N_EXPERTS, 1), lambda i: (0, 0)),
            pl.BlockSpec((tm, tm), lambda i: (0, 0)),
        ],
        out_specs=(
            pl.BlockSpec((tm, D_MODEL), lambda i: (i, 0)),
            pl.BlockSpec((TOP_K, tm), lambda i: (0, i)),
            pl.BlockSpec((TOP_K, tm), lambda i: (0, i)),
            pl.BlockSpec((TOP_K, tm), lambda i: (0, i)),
            pl.BlockSpec((N_EXPERTS, 1), lambda i: (0, 0)),
        ),
        scratch_shapes=[pltpu.VMEM((N_EXPERTS, 1), F32)],
        compiler_params=_params(dimension_semantics=("arbitrary",)),
        name="norm2_router",
    )(x, g.reshape(1, D_MODEL), mods, mods, w_router_t, b_router.reshape(N_EXPERTS, 1), tri)


def _moe_kernel(ue_ref, urow_ref, un_ref, uf_ref, ua_ref, src_ref,
                h_hbm, wg_ref, wl_ref, bg_ref, bl_ref, wd_ref, bd_ref, out_hbm,
                xf, xb, acc2, gsem, osem):
    u = pl.program_id(0)
    c = pl.program_id(1)
    nrows = un_ref[u]
    nfill = uf_ref[u]
    row0 = urow_ref[u]
    row0_next = urow_ref[jnp.minimum(u + 1, MOE_UNITS - 1)]

    def row_copy(r, tok):
        return pltpu.make_async_copy(h_hbm.at[pl.ds(tok, 1)], xf.at[pl.ds(r, 1)], gsem)

    def gather_sub(first_row, first_slot):
        for r in range(MOE_SUB):
            row_copy(first_slot + r, src_ref[first_row + r]).start()

    slot = u % 2
    acc = acc2.at[slot]

    def out_copy(which, first_row, sub):
        rows = pl.ds(sub * MOE_SUB, MOE_SUB)
        dst = pl.ds(pl.multiple_of(first_row + sub * MOE_SUB, MOE_SUB), MOE_SUB)
        return pltpu.make_async_copy(acc2.at[which, rows], out_hbm.at[dst], osem)

    @pl.when(c == 0)
    def _():
        @pl.when(u == 0)
        def _():
            for sub in range(MOE_NSUB):
                gather_sub(row0 + sub * MOE_SUB, sub * MOE_SUB)

        @pl.when(nrows > 0)
        def _():
            acc[...] = jnp.broadcast_to(bd_ref[...], acc.shape)

        @pl.when((u == 0) | (ua_ref[jnp.maximum(u - 1, 0)] > 0))
        def _():
            for r in range(MOE_UNIT):
                row_copy(r, 0).wait()

        for sub in range(MOE_NSUB):
            @pl.when(sub * MOE_SUB < nrows)
            def _():
                rows = pl.ds(sub * MOE_SUB, MOE_SUB)
                xb[rows, :] = xf[rows, :].astype(BF16)

    def expert_rows(sub, wg, wl, wd):
        rows = pl.ds(sub * MOE_SUB, MOE_SUB)
        xs = xb[rows, :]
        glu = jnp.dot(xs, wg, preferred_element_type=F32) + bg_ref[...]
        lin = jnp.dot(xs, wl, preferred_element_type=F32) + bl_ref[...]
        glu = jnp.minimum(glu, SWIGLU_LIMIT)
        lin = jnp.clip(lin, -SWIGLU_LIMIT, SWIGLU_LIMIT)
        hdn = glu * jax.nn.sigmoid(SWIGLU_ALPHA * glu) * (lin + 1.0)
        acc[rows, :] = acc[rows, :] + jnp.dot(hdn.astype(BF16), wd, preferred_element_type=F32)

    nsub = nrows // MOE_SUB
    for n in range(1, MOE_NSUB + 1):
        @pl.when(nsub == n)
        def _():
            wg = wg_ref[...].astype(BF16)
            wl = wl_ref[...].astype(BF16)
            wd = wd_ref[...].astype(BF16)
            gather_sub(row0_next + c * MOE_SUB, c * MOE_SUB)
            for sub in range(n):
                expert_rows(sub, wg, wl, wd)

    @pl.when(c == MOE_NCHUNK - 1)
    def _():
        @pl.when(nfill > 0)
        def _():
            acc[...] = jnp.zeros(acc.shape, F32)

        prev = jnp.maximum(u - 1, 0)
        nout_prev = jnp.where(u > 0, un_ref[prev] + uf_ref[prev], 0)
        for sub in range(MOE_NSUB):
            @pl.when(sub * MOE_SUB < nout_prev)
            def _():
                out_copy(1 - slot, urow_ref[prev], sub).wait()

        nout = nrows + nfill
        for sub in range(MOE_NSUB):
            @pl.when(sub * MOE_SUB < nout)
            def _():
                out_copy(slot, row0, sub).start()

        @pl.when(u == MOE_UNITS - 1)
        def _():
            for sub in range(MOE_NSUB):
                @pl.when(sub * MOE_SUB < nout)
                def _():
                    out_copy(slot, row0, sub).wait()


def _moe(h2, plan, w_gu, b_gu, w_down, b_down, layer):
    fc = MOE_FCHUNK

    def chunk(c, act, u):
        return jnp.where(act[u] > 0, c, MOE_NCHUNK - 1)

    def wg_map(u, c, ue, ur, un, uf, ua, sr):
        return (layer, ue[u], 0, chunk(c, ua, u))

    def wl_map(u, c, ue, ur, un, uf, ua, sr):
        return (layer, ue[u], 0, MOE_NCHUNK + chunk(c, ua, u))

    def wd_map(u, c, ue, ur, un, uf, ua, sr):
        return (layer, ue[u], chunk(c, ua, u), 0)

    def bd_map(u, c, ue, ur, un, uf, ua, sr):
        return (layer, ue[u], 0, 0)

    grid_spec = pltpu.PrefetchScalarGridSpec(
        num_scalar_prefetch=6,
        grid=(MOE_UNITS, MOE_NCHUNK),
        in_specs=[
            pl.BlockSpec(memory_space=pl.ANY),
            pl.BlockSpec((None, None, D_MODEL, fc), wg_map),
            pl.BlockSpec((None, None, D_MODEL, fc), wl_map),
            pl.BlockSpec((None, None, 1, fc), wg_map),
            pl.BlockSpec((None, None, 1, fc), wl_map),
            pl.BlockSpec((None, None, fc, D_MODEL), wd_map),
            pl.BlockSpec((None, None, 1, D_MODEL), bd_map),
        ],
        out_specs=pl.BlockSpec(memory_space=pl.ANY),
        scratch_shapes=[
            pltpu.VMEM((MOE_UNIT, D_MODEL), F32),
            pltpu.VMEM((MOE_UNIT, D_MODEL), BF16),
            pltpu.VMEM((2, MOE_UNIT, D_MODEL), F32),
            pltpu.SemaphoreType.DMA,
            pltpu.SemaphoreType.DMA,
        ],
    )
    return pl.pallas_call(
        _moe_kernel,
        out_shape=jax.ShapeDtypeStruct((MOE_ROWS, D_MODEL), F32),
        grid_spec=grid_spec,
        compiler_params=pltpu.CompilerParams(vmem_limit_bytes=MOE_VMEM_LIMIT,
                                             dimension_semantics=("arbitrary", "arbitrary")),
        name="moe_experts",
    )(*plan, h2, w_gu, w_gu,
      b_gu.reshape(DEPTH, N_EXPERTS, 1, 2 * D_FF), b_gu.reshape(DEPTH, N_EXPERTS, 1, 2 * D_FF),
      w_down, b_down.reshape(DEPTH, N_EXPERTS, 1, D_MODEL))


def _combine_kernel(dest_ref, rows_hbm, gate_ref, x_ref, g2_ref, *rest, split):
    o_refs, (buf, sem) = rest[:-2], rest[-2:]
    tm = COMBINE_TM
    base = pl.program_id(0) * tm

    def row_copy(r, kk, d):
        return pltpu.make_async_copy(rows_hbm.at[pl.ds(d, 1)], buf.at[kk, pl.ds(r, 1)], sem)

    for r in range(tm):
        for kk in range(TOP_K):
            row_copy(r, kk, dest_ref[(base + r) * TOP_K + kk]).start()
    for r in range(tm):
        for kk in range(TOP_K):
            row_copy(r, kk, 0).wait()
    gates = gate_ref[...]
    y = gates[:, 0:1] * buf[0]
    for kk in range(1, TOP_K):
        y = y + gates[:, kk:kk + 1] * buf[kk]
    val = x_ref[...] + g2_ref[...] * y
    if not split:
        o_refs[0][...] = val
    else:
        @pl.when(base < T_CTX)
        def _():
            o_refs[0][...] = val

        @pl.when(base >= T_CTX)
        def _():
            o_refs[1][...] = val


def _combine(dest_flat, rows, gates_t, x, mods, split):
    tm = COMBINE_TM
    n_ctx = T_CTX // tm
    if split:
        out_shape = (jax.ShapeDtypeStruct((T_CTX, D_MODEL), F32),
                     jax.ShapeDtypeStruct((T_LAT, D_MODEL), F32))
        out_specs = (pl.BlockSpec((tm, D_MODEL), lambda i, d: (jnp.minimum(i, n_ctx - 1), 0)),
                     pl.BlockSpec((tm, D_MODEL), lambda i, d: (jnp.maximum(i - n_ctx, 0), 0)))
    else:
        out_shape = jax.ShapeDtypeStruct((T_ALL, D_MODEL), F32)
        out_specs = pl.BlockSpec((tm, D_MODEL), lambda i, d: (i, 0))
    grid_spec = pltpu.PrefetchScalarGridSpec(
        num_scalar_prefetch=1,
        grid=(T_ALL // tm,),
        in_specs=[
            pl.BlockSpec(memory_space=pl.ANY),
            pl.BlockSpec((tm, TOP_K), lambda i, d: (i, 0)),
            pl.BlockSpec((tm, D_MODEL), lambda i, d: (i, 0)),
            pl.BlockSpec((None, 1, D_MODEL), lambda i, d: (_mod_row(i * tm), 0, 5)),
        ],
        out_specs=out_specs,
        scratch_shapes=[pltpu.VMEM((TOP_K, tm, D_MODEL), F32), pltpu.SemaphoreType.DMA],
    )
    return pl.pallas_call(
        functools.partial(_combine_kernel, split=split),
        out_shape=out_shape,
        grid_spec=grid_spec,
        compiler_params=_params(dimension_semantics=("arbitrary",)),
        name="moe_combine",
    )(dest_flat, rows, gates_t, x, mods)


def _src_kernel(dest_ref, src_ref):
    step = 8

    def clear(i, carry):
        for j in range(step):
            src_ref[i * step + j] = 0
        return carry

    def put(i, carry):
        for j in range(step):
            src_ref[dest_ref[i * step + j]] = i * (step // TOP_K) + j // TOP_K
        return carry

    lax.fori_loop(0, MOE_SRC_LEN // step, clear, 0)
    lax.fori_loop(0, N_ASSIGN // step, put, 0)


def _moe_src(dest_flat):
    return pl.pallas_call(
        _src_kernel,
        out_shape=jax.ShapeDtypeStruct((MOE_SRC_LEN,), jnp.int32),
        in_specs=[pl.BlockSpec(memory_space=pltpu.SMEM)],
        out_specs=pl.BlockSpec(memory_space=pltpu.SMEM),
        name="moe_src_rows",
    )(dest_flat)


def _moe_plan(e4, r4, counts):
    cnt = counts[:, 0].astype(jnp.int32)
    pad = (cnt + MOE_SUB - 1) // MOE_SUB * MOE_SUB
    pad_end = jnp.cumsum(pad)
    pad_start = pad_end - pad
    hot = e4[:, :, None] == jnp.arange(N_EXPERTS, dtype=jnp.int32)
    dest = jnp.sum(jnp.where(hot, pad_start, 0), axis=-1) + r4
    n_units_e = (pad + MOE_UNIT - 1) // MOE_UNIT
    unit_end = jnp.cumsum(n_units_e)
    unit_start = unit_end - n_units_e
    n_units = unit_end[-1]
    uid = jnp.minimum(jnp.arange(MOE_UNITS, dtype=jnp.int32), n_units - 1)
    unit_e = jnp.minimum(jnp.sum((uid[:, None] >= unit_end[None, :]).astype(jnp.int32), axis=1),
                         N_EXPERTS - 1)
    mine = unit_e[:, None] == jnp.arange(N_EXPERTS, dtype=jnp.int32)
    pick = lambda tab: jnp.sum(jnp.where(mine, tab, 0), axis=1)
    local = uid - pick(unit_start)
    unit_row = pick(pad_start) + local * MOE_UNIT
    unit_act = (jnp.arange(MOE_UNITS, dtype=jnp.int32) < n_units).astype(jnp.int32)
    unit_n = jnp.clip(pick(pad) - local * MOE_UNIT, 0, MOE_UNIT) * unit_act
    fill_row = pad_end[-1] + (jnp.arange(MOE_UNITS, dtype=jnp.int32) - n_units) * MOE_UNIT
    unit_fill = jnp.clip(MOE_ROWS - fill_row, 0, MOE_UNIT) * (1 - unit_act)
    unit_row = jnp.where(unit_act > 0, unit_row, jnp.minimum(fill_row, MOE_ROWS - MOE_SUB))
    dest_flat = dest.T.reshape(-1)
    i32 = lambda a: a.astype(jnp.int32)
    return dest_flat, (unit_e, i32(unit_row), i32(unit_n), i32(unit_fill), unit_act,
                       _moe_src(dest_flat))


def _dft_tables(n):
    idx = np.outer(np.arange(n), np.arange(n)) % n
    ang = 2.0 * np.pi * idx / n
    s = 1.0 / np.sqrt(n)
    return np.cos(ang) * s, np.sin(ang) * s


def _rope_tables():
    t = np.arange(LAT_SEQ)
    nf = HEAD_DIM // 4
    inv = ROPE_BASE ** (-np.arange(nf, dtype=np.float64) / nf)
    ang = np.concatenate([(t // GRID_W)[:, None] * inv, (t % GRID_W)[:, None] * inv], axis=-1)
    lane = np.arange(LANES)
    cos = np.cos(ang)[:, lane % (HEAD_DIM // 2)]
    sin = np.sin(ang)[:, lane % (HEAD_DIM // 2)]
    sign = np.where((lane % HEAD_DIM) < HEAD_DIM // 2, -1.0, 1.0)
    return cos.astype(np.float32), (sin * sign).astype(np.float32)


def _na_bias(rpb):
    c = np.arange(GRID_W)
    c0 = np.clip(c - NA_WIN_W // 2, 0, GRID_W - NA_WIN_W)
    c_ok = (c[None, :] >= c0[:, None]) & (c[None, :] < c0[:, None] + NA_WIN_W)
    dc = c[None, :] - c[:, None] + NA_WIN_W - 1
    pick = (np.arange(2 * NA_WIN_W - 1)[:, None, None] == dc[None]) & c_ok[None]
    cols = jnp.einsum('hrd,dqk->hrqk', rpb.astype(F32), jnp.asarray(pick, F32),
                      precision=lax.Precision.HIGHEST)
    cols = jnp.where(jnp.asarray(c_ok)[None, None], cols, NEG_INF)
    edge = jnp.full((NA_HEADS, 1, GRID_W, GRID_W), NEG_INF, F32)
    return jnp.concatenate([jnp.concatenate([edge, cols], axis=1),
                            jnp.concatenate([cols, edge], axis=1)], axis=-1)


def _dup_heads(cache):
    b, p = cache.shape[:2]
    return jnp.broadcast_to(cache[:, :, :, None, :], (b, p, SW_KV_HEADS, 2, HEAD_DIM)).reshape(
        b, p, SW_KV_HEADS * LANES)


def kernel(x_prompt, x_sample, cache_na_k, cache_na_v, cache_sw_k, cache_sw_v, c, c_ctx,
           w_ada, b_ada, norm1_g, norm2_g, w_in, qn_na, kn_na, qn_sw, kn_sw, rpb_na, sink_sw,
           w_br_na, w_br_ft, w_br_sw, w_o, w_router, b_router, w_gu, b_gu, w_down, b_down):
    nq = NA_HEADS * HEAD_DIM
    x = (x_prompt.reshape(T_CTX, D_MODEL), x_sample.reshape(T_LAT, D_MODEL))
    cvecs = jnp.zeros((MOD_ROWS, D_MODEL), F32).at[:LAT_BATCH].set(c).at[CTX_MOD_ROW].set(c_ctx)
    mods_all = _ada(cvecs, w_ada, b_ada)

    cc, sc = _dft_tables(FT_GROUP_DIM)
    cl_ctx, sl_ctx = _dft_tables(CTX_SEQ)
    cl_lat, sl_lat = _dft_tables(LAT_SEQ)
    as_bf16 = lambda a: jnp.asarray(a, BF16)
    cos_t, sin_t = _rope_tables()
    bd = as_bf16(np.kron(np.eye(LANES // HEAD_DIM), np.ones((HEAD_DIM, HEAD_DIM))))
    tri = as_bf16(np.triu(np.ones((ROUTER_TM, ROUTER_TM)), 1))
    tile2 = lambda g: jnp.tile(g.reshape(1, HEAD_DIM), (1, LANES // HEAD_DIM))

    caches = [jnp.zeros((CTX_BATCH, DEPTH, CTX_SEQ, w), F32)
              for w in (nq, nq, SW_KV_HEADS * HEAD_DIM, SW_KV_HEADS * HEAD_DIM)]
    for l in range(DEPTH):
        mods = mods_all[l].reshape(MOD_ROWS, 1, N_MOD)
        h = _prenorm(x, norm1_g[l], mods)
        z = _in_proj(h, w_in, l)
        qa, ka, qc, kc_dup, vc_dup, *caches = _prep(
            z, tile2(qn_na[l]), tile2(kn_na[l]), tile2(qn_sw[l]), tile2(kn_sw[l]), bd,
            jnp.asarray(cos_t), jnp.asarray(sin_t), caches, l)

        ctx = dict(n_batch=CTX_BATCH, seq=CTX_SEQ, row0=0, tq=CTX_SEQ, pp=8)
        lat = dict(n_batch=LAT_BATCH, seq=LAT_SEQ, row0=T_CTX, tq=256, pp=2)
        o_init = jnp.zeros((T_ALL, nq), BF16)
        o_a = _attention(qa, ka, z, v_col0=Z_VA // (LANES * ctx['pp']), out_prev=o_init,
                         name="attn_na_ctx", **ctx)
        o_a = _attention(qa, ka, z, v_col0=Z_VA // (LANES * lat['pp']),
                         cache_k=cache_na_k[:, l].reshape(LAT_BATCH, PAST_LEN, nq),
                         cache_v=cache_na_v[:, l].reshape(LAT_BATCH, PAST_LEN, nq),
                         bias=_na_bias(rpb_na[l]), kwin=NA_KEY_ROWS * GRID_W, out_prev=o_a,
                         name="attn_na_lat", **lat)
        o_c = _attention(qc, kc_dup, vc_dup, kv_div=2, sink=sink_sw[l], out_prev=o_init,
                         name="attn_sw_ctx", **ctx)
        o_c = _attention(qc, kc_dup, vc_dup, kv_div=2, sink=sink_sw[l],
                         cache_k=_dup_heads(cache_sw_k[:, l]), cache_v=_dup_heads(cache_sw_v[:, l]),
                         band=True, kwin=lat['tq'] + 2 * SW_WINDOW, out_prev=o_c, name="attn_sw_lat", **lat)
        as_f32 = lambda a: jnp.asarray(a, F32)
        o_b = _fourier(z, as_f32(cc), as_f32(-sc), as_f32(cl_ctx), as_f32(sl_ctx),
                       n_batch=CTX_BATCH, seq=CTX_SEQ, row0=0, gps=FT_GROUPS, out_prev=o_init,
                       name="fourier_ctx")
        o_b = _fourier(z, as_f32(cc), as_f32(-sc), as_f32(cl_lat), as_f32(sl_lat),
                       n_batch=LAT_BATCH, seq=LAT_SEQ, row0=T_CTX, gps=1, out_prev=o_b,
                       name="fourier_lat")
        m = _merge1(o_a, o_b, o_c, w_br_na, w_br_ft, w_br_sw, z, l)
        x = _merge2(m, w_o, x, mods, l)

        h2, e4, g4, r4, counts = _router(x, norm2_g[l], mods, w_router[l].T, b_router[l], tri)
        dest_flat, plan = _moe_plan(e4, r4, counts)
        rows = _moe(h2, plan, w_gu, b_gu, w_down, b_down, l)
        x = _combine(dest_flat, rows, g4.T, x, mods, split=(l == DEPTH - 1))

    y_prompt = x[0].reshape(CTX_BATCH, CTX_SEQ, D_MODEL)
    y_sample = x[1].reshape(LAT_BATCH, LAT_SEQ, D_MODEL)
    heads = lambda c, n: c.reshape(CTX_BATCH, DEPTH, CTX_SEQ, n, HEAD_DIM)
    return (y_prompt, y_sample, heads(caches[0], NA_HEADS), heads(caches[1], NA_HEADS),
            heads(caches[2], SW_KV_HEADS), heads(caches[3], SW_KV_HEADS))
```
